```python
import math
import jax, jax.numpy as jnp
from jax import lax
import numpy as np

D_MODEL = 1024
BATCH = 1
SEQ = 16384
DEPTH = 2
DEC_BATCH = 32
DEC_SEQ = 16
PAST_LEN = 1024

CHUNK = 64
Q_BLOCK = 128
N_A = DEPTH // 2
N_B = DEPTH - N_A
EPS = 1e-6
M_EXPAND = 2
D_INNER = M_EXPAND * D_MODEL
M_HEAD_DIM = 64
M_HEADS = D_INNER // M_HEAD_DIM
M_GROUPS = 4
D_STATE = 128
M_CONV_W = 4
CONV_DIM = D_INNER + 2 * M_GROUPS * D_STATE
IN_PROJ_DIM = D_INNER + CONV_DIM + M_HEADS
HEAD_DIM = 64
N_HEADS = D_MODEL // HEAD_DIM
ATTN_SCALE = HEAD_DIM ** -0.5
FORGET_BIAS_INIT = 3.0
D_FF = 2816
F_CONV_W = 3

kernel_name = 'hybrid_ssd_fox_yoco_stream_step'


def rmsnorm(x, w):
    xf = x.astype(jnp.float32)
    y = xf * lax.rsqrt(jnp.mean(xf * xf, axis=-1, keepdims=True) + EPS)
    return (y * w.astype(jnp.float32)).astype(x.dtype)


def causal_dwconv(u, prev, w, b):
    width = w.shape[0]
    l = u.shape[1]
    full = jnp.concatenate([prev.astype(u.dtype), u], axis=1)
    out = b
    for tap in range(width):
        out = out + full[:, tap:tap + l] * w[tap]
    return out, full[:, full.shape[1] - (width - 1):]


def ssd_scan(x, dt, A, B, C, h0):
    b, l, h, p = x.shape
    g, n = B.shape[2], B.shape[3]
    r = h // g
    L = min(CHUNK, l)
    nc = l // L
    f32 = jnp.float32
    xf = x.astype(f32).reshape(b, nc, L, g, r, p)
    dtb = dt.reshape(b, nc, L, g, r)
    Bb = B.astype(f32).reshape(b, nc, L, g, n)
    Cb = C.astype(f32).reshape(b, nc, L, g, n)
    acum = jnp.cumsum(dtb * A.reshape(g, r), axis=2)
    causal = jnp.tril(jnp.ones((L, L), dtype=bool))
    seg = acum[:, :, :, None] - acum[:, :, None, :]
    decay = jnp.exp(jnp.where(causal[:, :, None, None], seg, -jnp.inf))
    cb = jnp.einsum('bctgn,bcsgn->bctsg', Cb, Bb)
    w_ts = cb[..., None] * decay * dtb[:, :, None]
    y_intra = jnp.einsum('bctsgr,bcsgrp->bctgrp', w_ts, xf)
    to_end = jnp.exp(acum[:, :, -1:] - acum) * dtb
    st = jnp.einsum('bcsgn,bcsgrp->bcgrpn', Bb, to_end[..., None] * xf)
    blk_decay = jnp.exp(acum[:, :, -1])

    def step(hc, inp):
        s_c, d_c = inp
        return hc * d_c[..., None, None] + s_c, hc

    h_last, h_in = lax.scan(step, h0.astype(f32).reshape(b, g, r, p, n),
                            (jnp.moveaxis(st, 1, 0), jnp.moveaxis(blk_decay, 1, 0)))
    h_in = jnp.moveaxis(h_in, 0, 1)
    y_inter = jnp.einsum('bctgn,bcgrpn->bctgrp', Cb, h_in) * jnp.exp(acum)[..., None]
    y = (y_intra + y_inter).reshape(b, l, h, p)
    return y.astype(x.dtype), h_last.reshape(b, h, p, n).astype(h0.dtype)


def mamba_mixer(u, ssm0, conv0, w_in, conv_w, conv_b, dt_bias, A_log, Dp, norm_w, w_out):
    b, l, _ = u.shape
    zxbcdt = u @ w_in
    z = zxbcdt[..., :D_INNER]
    xbc = zxbcdt[..., D_INNER:D_INNER + CONV_DIM]
    dt_raw = zxbcdt[..., D_INNER + CONV_DIM:]
    xbc, conv_new = causal_dwconv(xbc, conv0, conv_w, conv_b)
    xbc = jax.nn.silu(xbc)
    xs = xbc[..., :D_INNER].reshape(b, l, M_HEADS, M_HEAD_DIM)
    Bm = xbc[..., D_INNER:D_INNER + M_GROUPS * D_STATE].reshape(b, l, M_GROUPS, D_STATE)
    Cm = xbc[..., D_INNER + M_GROUPS * D_STATE:].reshape(b, l, M_GROUPS, D_STATE)
    dt = jax.nn.softplus((dt_raw + dt_bias).astype(jnp.float32))
    A = -jnp.exp(A_log.astype(jnp.float32))
    y, ssm_new = ssd_scan(xs, dt, A, Bm, Cm, ssm0)
    y = (y + Dp[:, None] * xs).reshape(b, l, D_INNER) * jax.nn.silu(z)
    gs = D_INNER // M_GROUPS
    y = rmsnorm(y.reshape(b, l, M_GROUPS, gs), norm_w.reshape(M_GROUPS, gs)).reshape(b, l, D_INNER)
    return y @ w_out, ssm_new, conv_new


def fox_attend(q, k, v, cq, ck, q_pos, k_pos):
    s = jnp.einsum('bqhd,bkhd->bhqk', q, k).astype(jnp.float32) * ATTN_SCALE
    s = s + jnp.swapaxes(cq, 1, 2)[..., :, None] - jnp.swapaxes(ck, 1, 2)[..., None, :]
    s = jnp.where(q_pos[:, None] >= k_pos[None, :], s, -jnp.inf)
    pr = jax.nn.softmax(s, axis=-1).astype(v.dtype)
    return jnp.einsum('bhqk,bkhd->bqhd', pr, v)


def fox_mixer(u, k_all, v_all, c_all, p0, w_q, w_o):
    b, l, _ = u.shape
    q = (u @ w_q).reshape(b, l, N_HEADS, HEAD_DIM)
    c_q = c_all[:, p0:]
    k_pos = jnp.arange(k_all.shape[1])
    q_pos = p0 + jnp.arange(l)
    if l % Q_BLOCK == 0:
        nq = l // Q_BLOCK
        qb = jnp.swapaxes(q.reshape(b, nq, Q_BLOCK, N_HEADS, HEAD_DIM), 0, 1)
        cqb = jnp.swapaxes(c_q.reshape(b, nq, Q_BLOCK, N_HEADS), 0, 1)
        pb = q_pos.reshape(nq, Q_BLOCK)
        o = lax.map(lambda blk: fox_attend(blk[0], k_all, v_all, blk[1], c_all, blk[2], k_pos), (qb, cqb, pb))
        o = jnp.swapaxes(o, 0, 1).reshape(b, l, N_HEADS * HEAD_DIM)
    else:
        o = fox_attend(q, k_all, v_all, c_q, c_all, q_pos, k_pos).reshape(b, l, N_HEADS * HEAD_DIM)
    return o @ w_o


def conv_ffn(u, prev, w_up, conv_w, conv_b, w_down):
    up, buf_new = causal_dwconv(u @ w_up, prev, conv_w, conv_b)
    gate = up[..., :D_FF]
    val = up[..., D_FF:]
    return (jax.nn.silu(gate) * val) @ w_down, buf_new


def trunk(x, ssm0, mconv0, fconv0, past_k, past_v, past_logf,
          norm_mix_w, norm_ffn_w, mamba_w_in, mamba_conv_w, mamba_conv_b, mamba_dt_bias,
          mamba_A_log, mamba_D, mamba_norm_w, mamba_w_out, kv_norm_w, w_k, w_v, w_f, b_f,
          attn_w_q, attn_w_o, ffn_w_up, ffn_conv_w, ffn_conv_b, ffn_w_down, final_norm_w):
    b, l, _ = x.shape
    p0 = past_k.shape[1]
    h = x
    ssm_out, mconv_out, fconv_out = [], [], []
    k_new = v_new = logf_new = None
    for i in range(DEPTH):
        if i == N_A:
            hk = rmsnorm(h, kv_norm_w)
            k_new = (hk @ w_k).reshape(b, l, N_HEADS, HEAD_DIM)
            v_new = (hk @ w_v).reshape(b, l, N_HEADS, HEAD_DIM)
            logf_new = jax.nn.log_sigmoid((hk @ w_f + b_f).astype(jnp.float32))
            k_all = jnp.concatenate([past_k.astype(k_new.dtype), k_new], axis=1)
            v_all = jnp.concatenate([past_v.astype(v_new.dtype), v_new], axis=1)
            c_all = jnp.cumsum(jnp.concatenate([past_logf.astype(jnp.float32), logf_new], axis=1), axis=1)
        u = rmsnorm(h, norm_mix_w[i])
        if i < N_A:
            mix, s_new, c_new = mamba_mixer(u, ssm0[i], mconv0[i], mamba_w_in[i], mamba_conv_w[i],
                                            mamba_conv_b[i], mamba_dt_bias[i], mamba_A_log[i],
                                            mamba_D[i], mamba_norm_w[i], mamba_w_out[i])
            ssm_out.append(s_new)
            mconv_out.append(c_new)
        else:
            j = i - N_A
            mix = fox_mixer(u, k_all, v_all, c_all, p0, attn_w_q[j], attn_w_o[j])
        h = h + mix
        f, fc_new = conv_ffn(rmsnorm(h, norm_ffn_w[i]), fconv0[i], ffn_w_up[i], ffn_conv_w[i],
                             ffn_conv_b[i], ffn_w_down[i])
        fconv_out.append(fc_new)
        h = h + f
    y = rmsnorm(h, final_norm_w)
    return (y, jnp.stack(ssm_out), jnp.stack(mconv_out), jnp.stack(fconv_out),
            k_new, v_new, logf_new.astype(past_logf.dtype))


def setup_inputs(seed: int = 0) -> dict:
    key = jax.random.key(seed)
    ks = jax.random.split(key, 32)
    f32 = jnp.float32

    def nrm(k, shape, scale=1.0):
        return jax.random.normal(k, shape, f32) * scale

    dt0 = jnp.exp(jax.random.uniform(ks[13], (N_A, M_HEADS), f32, math.log(1e-3), math.log(1e-1)))
    return {
        'x_prompt': nrm(ks[0], (BATCH, SEQ, D_MODEL)),
        'x_sample': nrm(ks[1], (DEC_BATCH, DEC_SEQ, D_MODEL)),
        'state_ssm': nrm(ks[2], (N_A, DEC_BATCH, M_HEADS, M_HEAD_DIM, D_STATE), 0.1),
        'state_mamba_conv': nrm(ks[3], (N_A, DEC_BATCH, M_CONV_W - 1, CONV_DIM)),
        'state_ffn_conv': nrm(ks[4], (DEPTH, DEC_BATCH, F_CONV_W - 1, 2 * D_FF)),
        'cache_k': nrm(ks[5], (DEC_BATCH, PAST_LEN, N_HEADS, HEAD_DIM)),
        'cache_v': nrm(ks[6], (DEC_BATCH, PAST_LEN, N_HEADS, HEAD_DIM)),
        'cache_logf': jax.nn.log_sigmoid(FORGET_BIAS_INIT + nrm(ks[7], (DEC_BATCH, PAST_LEN, N_HEADS))),
        'norm_mix_w': 1.0 + nrm(ks[8], (DEPTH, D_MODEL), 0.02),
        'norm_ffn_w': 1.0 + nrm(ks[9], (DEPTH, D_MODEL), 0.02),
        'mamba_w_in': nrm(ks[10], (N_A, D_MODEL, IN_PROJ_DIM), D_MODEL ** -0.5),
        'mamba_conv_w': nrm(ks[11], (N_A, M_CONV_W, CONV_DIM), M_CONV_W ** -0.5),
        'mamba_conv_b': nrm(ks[12], (N_A, CONV_DIM), 0.02),
        'mamba_dt_bias': dt0 + jnp.log(-jnp.expm1(-dt0)),
        'mamba_A_log': jnp.log(jax.random.uniform(ks[14], (N_A, M_HEADS), f32, 1.0, 16.0)),
        'mamba_D': 1.0 + nrm(ks[15], (N_A, M_HEADS), 0.02),
        'mamba_norm_w': 1.0 + nrm(ks[16], (N_A, D_INNER), 0.02),
        'mamba_w_out': nrm(ks[17], (N_A, D_INNER, D_MODEL), D_INNER ** -0.5),
        'kv_norm_w': 1.0 + nrm(ks[18], (D_MODEL,), 0.02),
        'w_k': nrm(ks[19], (D_MODEL, N_HEADS * HEAD_DIM), D_MODEL ** -0.5),
        'w_v': nrm(ks[20], (D_MODEL, N_HEADS * HEAD_DIM), D_MODEL ** -0.5),
        'w_f': nrm(ks[21], (D_MODEL, N_HEADS), D_MODEL ** -0.5),
        'b_f': FORGET_BIAS_INIT + nrm(ks[22], (N_HEADS,), 0.1),
        'attn_w_q': nrm(ks[23], (N_B, D_MODEL, N_HEADS * HEAD_DIM), D_MODEL ** -0.5),
        'attn_w_o': nrm(ks[24], (N_B, N_HEADS * HEAD_DIM, D_MODEL), (N_HEADS * HEAD_DIM) ** -0.5),
        'ffn_w_up': nrm(ks[25], (DEPTH, D_MODEL, 2 * D_FF), D_MODEL ** -0.5),
        'ffn_conv_w': nrm(ks[26], (DEPTH, F_CONV_W, 2 * D_FF), F_CONV_W ** -0.5),
        'ffn_conv_b': nrm(ks[27], (DEPTH, 2 * D_FF), 0.02),
        'ffn_w_down': nrm(ks[28], (DEPTH, D_FF, D_MODEL), D_FF ** -0.5),
        'final_norm_w': 1.0 + nrm(ks[29], (D_MODEL,), 0.02),
    }


def reference(x_prompt, x_sample, state_ssm, state_mamba_conv, state_ffn_conv, cache_k, cache_v, cache_logf,
              norm_mix_w, norm_ffn_w, mamba_w_in, mamba_conv_w, mamba_conv_b, mamba_dt_bias, mamba_A_log,
              mamba_D, mamba_norm_w, mamba_w_out, kv_norm_w, w_k, w_v, w_f, b_f, attn_w_q, attn_w_o,
              ffn_w_up, ffn_conv_w, ffn_conv_b, ffn_w_down, final_norm_w):
    bp = x_prompt.shape[0]
    z_ssm = jnp.zeros((N_A, bp) + state_ssm.shape[2:], state_ssm.dtype)
    z_mconv = jnp.zeros((N_A, bp) + state_mamba_conv.shape[2:], state_mamba_conv.dtype)
    z_fconv = jnp.zeros((DEPTH, bp) + state_ffn_conv.shape[2:], state_ffn_conv.dtype)
    e_k = jnp.zeros((bp, 0, N_HEADS, HEAD_DIM), cache_k.dtype)
    e_v = jnp.zeros((bp, 0, N_HEADS, HEAD_DIM), cache_v.dtype)
    e_logf = jnp.zeros((bp, 0, N_HEADS), cache_logf.dtype)
    (y_prompt, p_ssm, p_mconv, p_fconv, p_k, p_v, p_logf) = trunk(
        x_prompt, z_ssm, z_mconv, z_fconv, e_k, e_v, e_logf,
        norm_mix_w, norm_ffn_w, mamba_w_in, mamba_conv_w, mamba_conv_b, mamba_dt_bias, mamba_A_log,
        mamba_D, mamba_norm_w, mamba_w_out, kv_norm_w, w_k, w_v, w_f, b_f, attn_w_q, attn_w_o,
        ffn_w_up, ffn_conv_w, ffn_conv_b, ffn_w_down, final_norm_w)
    (y_sample, s_ssm, s_mconv, s_fconv, s_k, s_v, s_logf) = trunk(
        x_sample, state_ssm, state_mamba_conv, state_ffn_conv, cache_k, cache_v, cache_logf,
        norm_mix_w, norm_ffn_w, mamba_w_in, mamba_conv_w, mamba_conv_b, mamba_dt_bias, mamba_A_log,
        mamba_D, mamba_norm_w, mamba_w_out, kv_norm_w, w_k, w_v, w_f, b_f, attn_w_q, attn_w_o,
        ffn_w_up, ffn_conv_w, ffn_conv_b, ffn_w_down, final_norm_w)
    return (y_prompt, y_sample, p_ssm, p_mconv, p_fconv, p_k, p_v, p_logf,
            s_ssm, s_mconv, s_fconv, s_k, s_v, s_logf)
```

```python
import functools

import jax
import jax.numpy as jnp
from jax import lax
from jax.experimental import pallas as pl
from jax.experimental.pallas import tpu as pltpu

F32 = jnp.float32
BF16 = jnp.bfloat16

EPS = 1e-6
D_MODEL = 1024
D_INNER = 2048
M_HEAD_DIM = 64
M_HEADS = 32
M_GROUPS = 4
HEADS_PER_GROUP = M_HEADS // M_GROUPS
D_STATE = 128
M_CONV_W = 4
CONV_DIM = D_INNER + 2 * M_GROUPS * D_STATE
IN_PROJ_DIM = D_INNER + CONV_DIM + M_HEADS
LANE = 128
IN_PROJ_PAD = -(-IN_PROJ_DIM // LANE) * LANE
HEAD_DIM = 64
N_HEADS = 16
D_ATTN = N_HEADS * HEAD_DIM
ATTN_SCALE = HEAD_DIM ** -0.5
D_FF = 2816
F_CONV_W = 3
HALO = 8
AUG = 2 * HEAD_DIM
VMEM_LIMIT = 56 * 1024 * 1024


def _dot(a, b):
    return jnp.dot(a, b, preferred_element_type=F32)


def _dot_nt(a, b):
    return lax.dot_general(a, b, (((1,), (1,)), ((), ())), preferred_element_type=F32)


def _dot_tn(a, b):
    return lax.dot_general(a, b, (((0,), (0,)), ((), ())), preferred_element_type=F32)


def _split3(x):
    hi = x.astype(BF16)
    r = x - hi.astype(F32)
    mid = r.astype(BF16)
    lo = (r - mid.astype(F32)).astype(BF16)
    return hi, mid, lo


def _sel_left(sel, x):
    hi, mid, lo = _split3(x)
    return _dot(sel, hi) + _dot(sel, mid) + _dot(sel, lo)


def _sel_right(x, sel):
    hi, mid, lo = _split3(x)
    return _dot(hi, sel) + _dot(mid, sel) + _dot(lo, sel)


def _sel_right_nt(sel, x):
    hi, mid, lo = _split3(x)
    return _dot_nt(sel, hi) + _dot_nt(sel, mid) + _dot_nt(sel, lo)


def _rmsnorm(x, w):
    return x * lax.rsqrt(jnp.mean(x * x, axis=-1, keepdims=True) + EPS) * w


def _silu(x):
    return x * (1.0 / (1.0 + jnp.exp(-x)))


def _softplus(x):
    return jnp.maximum(x, 0.0) + jnp.log1p(jnp.exp(-jnp.abs(x)))


def _tri(n, lower):
    r = lax.broadcasted_iota(jnp.int32, (n, n), 0)
    c = lax.broadcasted_iota(jnp.int32, (n, n), 1)
    return (r >= c) if lower else (r <= c)


def _const_spec(shape):
    nd = len(shape)
    return pl.BlockSpec(shape, lambda *_: (0,) * nd, pipeline_mode=pl.Buffered(1))


def _params(n_axes=2):
    return pltpu.CompilerParams(dimension_semantics=("arbitrary",) * n_axes,
                                vmem_limit_bytes=VMEM_LIMIT)


def _mamba_kernel(x_ref, conv0_ref, ssm0_ref, nw_ref, win_ref, wdtT_ref, cw_ref, cb_ref,
                  dtb_ref, dtbT_ref, alog_ref, alogT_ref, dx_ref, gnw_ref, wout_ref, ex_ref,
                  h_ref, convn_ref, ssmn_ref, fb_ref, y_ref, *, nb, L, Lc):
    M = nb * L
    lo = HALO - (M_CONV_W - 1)

    @pl.when(pl.program_id(1) == 0)
    def _():
        fb_ref[:, lo:HALO, :] = conv0_ref[...]
        ssmn_ref[...] = ssm0_ref[...]

    x = x_ref[...].reshape(M, D_MODEL)
    ub = _rmsnorm(x, nw_ref[...]).astype(BF16)
    zx = _dot(ub, win_ref[...])
    dtT = _softplus(_dot_nt(wdtT_ref[...], ub) + dtbT_ref[...])
    dt = _softplus(zx[:, D_INNER + CONV_DIM:IN_PROJ_DIM] + dtb_ref[...])

    fb_ref[:, HALO:HALO + L, :] = zx[:, D_INNER:D_INNER + CONV_DIM].reshape(nb, L, CONV_DIM)
    acc = cb_ref[...]
    for tap in range(M_CONV_W):
        acc = acc + fb_ref[:, lo + tap:lo + tap + L, :] * cw_ref[tap:tap + 1, :]
    halo = fb_ref[:, lo + L:HALO + L, :]
    fb_ref[:, lo:HALO, :] = halo
    convn_ref[...] = halo
    xbc = _silu(acc).reshape(M, CONV_DIM)
    xs = xbc[:, :D_INNER]
    bm = xbc[:, D_INNER:D_INNER + M_GROUPS * D_STATE].astype(BF16)
    cm = xbc[:, D_INNER + M_GROUPS * D_STATE:].astype(BF16)

    dA = dt * (-jnp.exp(alog_ref[...]))
    dAT = dtT * (-jnp.exp(alogT_ref[...]))
    tril = _tri(Lc, True)
    tril_b = tril.astype(BF16)
    triu_b = _tri(Lc, False).astype(BF16)
    ex = ex_ref[...]
    quad = lax.broadcasted_iota(jnp.int32, (Lc, 4 * M_HEAD_DIM), 1) // M_HEAD_DIM

    for s in range(nb):
        for c in range(L // Lc):
            r0 = s * L + c * Lc
            acum = _sel_left(tril_b, dA[r0:r0 + Lc])
            dtT_c = dtT[:, r0:r0 + Lc]
            acumT = _sel_right(dAT[:, r0:r0 + Lc], triu_b)
            acum_x = _sel_right(acum, ex)
            dt_x = _sel_right(dt[r0:r0 + Lc], ex)
            xs_c = xs[r0:r0 + Lc]
            xs_b = xs_c.astype(BF16)
            x_end = (xs_c * (jnp.exp(acum_x[Lc - 1:Lc, :] - acum_x) * dt_x)).astype(BF16)
            e_x = jnp.exp(acum_x)
            d_blk = jnp.exp(acumT[:, Lc - 1:Lc])
            for g in range(M_GROUPS):
                b_g = bm[r0:r0 + Lc, g * D_STATE:(g + 1) * D_STATE]
                c_g = cm[r0:r0 + Lc, g * D_STATE:(g + 1) * D_STATE]
                cb = _dot_nt(c_g, b_g)
                h0 = g * HEADS_PER_GROUP
                h_in = ssmn_ref[s, h0:h0 + HEADS_PER_GROUP].reshape(HEADS_PER_GROUP * M_HEAD_DIM, D_STATE)
                y_inter = _dot_nt(c_g, h_in.astype(BF16))
                for qd in range(HEADS_PER_GROUP // 4):
                    col = (h0 + 4 * qd) * M_HEAD_DIM
                    xq = xs_b[:, col:col + 4 * M_HEAD_DIM]
                    yq = jnp.zeros((Lc, 4 * M_HEAD_DIM), F32)
                    for j in range(4):
                        h = h0 + 4 * qd + j
                        seg = acum[:, h:h + 1] - acumT[h:h + 1, :]
                        w = cb * jnp.exp(jnp.where(tril, seg, -jnp.inf)) * dtT_c[h:h + 1, :]
                        yq = yq + _dot(w.astype(BF16), jnp.where(quad == j, xq, jnp.zeros_like(xq)))
                    y_ref[r0:r0 + Lc, col:col + 4 * M_HEAD_DIM] = (
                        yq + y_inter[:, qd * 4 * M_HEAD_DIM:(qd + 1) * 4 * M_HEAD_DIM]
                        * e_x[:, col:col + 4 * M_HEAD_DIM]
                        + dx_ref[:, col:col + 4 * M_HEAD_DIM] * xs_c[:, col:col + 4 * M_HEAD_DIM])
                st = _dot_tn(x_end[:, h0 * M_HEAD_DIM:(h0 + HEADS_PER_GROUP) * M_HEAD_DIM], b_g)
                for j in range(HEADS_PER_GROUP):
                    h = h0 + j
                    ssmn_ref[s, h] = (ssmn_ref[s, h] * d_blk[h:h + 1, :]
                                      + st[j * M_HEAD_DIM:(j + 1) * M_HEAD_DIM, :])

    yz = y_ref[...] * _silu(zx[:, :D_INNER])
    gs = D_INNER // M_GROUPS
    parts = []
    for g in range(M_GROUPS):
        blk = yz[:, g * gs:(g + 1) * gs]
        parts.append(_rmsnorm(blk, gnw_ref[:, g * gs:(g + 1) * gs]))
    yn = jnp.concatenate(parts, axis=1).astype(BF16)
    h_ref[...] = (x + _dot(yn, wout_ref[...])).reshape(nb, L, D_MODEL)


def _mamba_layer(x, conv0, ssm0, w, *, nb, L, Lc):
    B, Ltot, _ = x.shape
    grid = (B // nb, Ltot // L)
    consts = [w["norm"], w["w_in"], w["w_dtT"], w["conv_w"], w["conv_b"], w["dt_b"], w["dt_bT"],
              w["a_log"], w["a_logT"], w["d_x"], w["gn_w"], w["w_out"], w["ex"]]
    return pl.pallas_call(
        functools.partial(_mamba_kernel, nb=nb, L=L, Lc=Lc),
        grid=grid,
        in_specs=[pl.BlockSpec((nb, L, D_MODEL), lambda s, t: (s, t, 0)),
                  pl.BlockSpec((nb, M_CONV_W - 1, CONV_DIM), lambda s, t: (s, 0, 0)),
                  pl.BlockSpec((nb, M_HEADS, M_HEAD_DIM, D_STATE), lambda s, t: (s, 0, 0, 0))]
        + [_const_spec(c.shape) for c in consts],
        out_specs=[pl.BlockSpec((nb, L, D_MODEL), lambda s, t: (s, t, 0)),
                   pl.BlockSpec((nb, M_CONV_W - 1, CONV_DIM), lambda s, t: (s, 0, 0)),
                   pl.BlockSpec((nb, M_HEADS, M_HEAD_DIM, D_STATE), lambda s, t: (s, 0, 0, 0))],
        out_shape=[jax.ShapeDtypeStruct(x.shape, F32),
                   jax.ShapeDtypeStruct(conv0.shape, F32),
                   jax.ShapeDtypeStruct(ssm0.shape, F32)],
        scratch_shapes=[pltpu.VMEM((nb, HALO + L, CONV_DIM), F32),
                        pltpu.VMEM((nb * L, D_INNER), F32)],
        compiler_params=_params(),
        name="mamba_layer",
    )(x, conv0, ssm0, *consts)


def _ffn_kernel(*refs, nb, L, with_attn, final):
    it = iter(refs)
    h_ref = next(it)
    if with_attn:
        o_ref, wo_ref = next(it), next(it)
    fconv0_ref, nw_ref, wup_ref, cw_ref, cb_ref, wdown_ref = (next(it) for _ in range(6))
    if final:
        fnw_ref = next(it)
    out_ref, fconvn_ref, fb_ref = next(it), next(it), next(it)
    M = nb * L
    lo = HALO - (F_CONV_W - 1)

    @pl.when(pl.program_id(1) == 0)
    def _():
        fb_ref[:, lo:HALO, :] = fconv0_ref[...]

    h = h_ref[...].reshape(M, D_MODEL)
    if with_attn:
        h = h + _dot(o_ref[...].reshape(M, D_ATTN), wo_ref[...])
    ub = _rmsnorm(h, nw_ref[...]).astype(BF16)
    fb_ref[:, HALO:HALO + L, :] = _dot(ub, wup_ref[...]).reshape(nb, L, 2 * D_FF)
    acc = cb_ref[...]
    for tap in range(F_CONV_W):
        acc = acc + fb_ref[:, lo + tap:lo + tap + L, :] * cw_ref[tap:tap + 1, :]
    halo = fb_ref[:, lo + L:HALO + L, :]
    fb_ref[:, lo:HALO, :] = halo
    fconvn_ref[...] = halo
    a = acc.reshape(M, 2 * D_FF)
    act = (_silu(a[:, :D_FF]) * a[:, D_FF:]).astype(BF16)
    hn = h + _dot(act, wdown_ref[...])
    if final:
        hn = _rmsnorm(hn, fnw_ref[...])
    out_ref[...] = hn.reshape(nb, L, D_MODEL)


def _ffn_layer(h, fconv0, w, *, nb, L, o=None, w_o=None, final_w=None):
    B, Ltot, _ = h.shape
    grid = (B // nb, Ltot // L)
    with_attn, final = o is not None, final_w is not None
    tile = pl.BlockSpec((nb, L, D_MODEL), lambda s, t: (s, t, 0))
    args, specs = [h], [tile]
    if with_attn:
        args += [o, w_o]
        specs += [pl.BlockSpec((nb, L, D_ATTN), lambda s, t: (s, t, 0)), _const_spec(w_o.shape)]
    consts = [w["norm"], w["w_up"], w["conv_w"], w["conv_b"], w["w_down"]]
    args += [fconv0] + consts
    specs += [pl.BlockSpec((nb, F_CONV_W - 1, 2 * D_FF), lambda s, t: (s, 0, 0))]
    specs += [_const_spec(c.shape) for c in consts]
    if final:
        args.append(final_w)
        specs.append(_const_spec(final_w.shape))
    return pl.pallas_call(
        functools.partial(_ffn_kernel, nb=nb, L=L, with_attn=with_attn, final=final),
        grid=grid,
        in_specs=specs,
        out_specs=[tile, pl.BlockSpec((nb, F_CONV_W - 1, 2 * D_FF), lambda s, t: (s, 0, 0))],
        out_shape=[jax.ShapeDtypeStruct(h.shape, F32), jax.ShapeDtypeStruct(fconv0.shape, F32)],
        scratch_shapes=[pltpu.VMEM((nb, HALO + L, 2 * D_FF), F32)],
        compiler_params=_params(),
        name="conv_ffn",
    )(*args)


def _log_sigmoid(x):
    return -_softplus(-x)


def _proj_prompt_kernel(h_ref, kvw_ref, nmw_ref, wk_ref, wv_ref, wvT_ref, wf_ref, bf_ref,
                        wkp_ref, wqp_ref, pk_ref, pq_ref, onek_ref, oneq_ref,
                        k_ref, v_ref, logf_ref, qa_ref, ka_ref, vT_ref, carry_ref, *, T):
    @pl.when(pl.program_id(0) == 0)
    def _():
        carry_ref[...] = jnp.zeros_like(carry_ref)

    h = h_ref[...]
    hk = _rmsnorm(h, kvw_ref[...]).astype(BF16)
    u1 = _rmsnorm(h, nmw_ref[...]).astype(BF16)
    k_ref[...] = _dot(hk, wk_ref[...])
    v_ref[...] = _dot(hk, wv_ref[...])
    vT_ref[0] = _dot_nt(wvT_ref[...], hk).astype(BF16)
    logf = _log_sigmoid(_dot(hk, wf_ref[...])[:, :N_HEADS] + bf_ref[...])
    logf_ref[...] = logf
    c = carry_ref[...] + _sel_left(_tri(T, True).astype(BF16), logf)
    carry_ref[...] = c[T - 1:T, :]
    c_hi, c_mid, c_lo = _split3(c)
    ka = _dot(hk, wkp_ref[...]) + onek_ref[...]
    qa = _dot(u1, wqp_ref[...]) + oneq_ref[...]
    for i, part in enumerate((c_hi, c_mid, c_lo)):
        ka = ka + _dot(part, pk_ref[i])
        qa = qa + _dot(part, pq_ref[i])
    ka_ref[...] = ka.astype(BF16)
    qa_ref[...] = qa.astype(BF16)


def _proj_prompt(h2d, w, *, T):
    Ltot = h2d.shape[0]
    consts = [w["kv_norm"], w["mix_norm"], w["w_k"], w["w_v"], w["w_vT"], w["w_f"], w["b_f"],
              w["w_k_aug"], w["w_q_aug"], w["p_k"], w["p_q"], w["one_k"], w["one_q"]]
    row = lambda n: pl.BlockSpec((T, n), lambda i: (i, 0))
    return pl.pallas_call(
        functools.partial(_proj_prompt_kernel, T=T),
        grid=(Ltot // T,),
        in_specs=[row(D_MODEL)] + [_const_spec(c.shape) for c in consts],
        out_specs=[row(D_ATTN), row(D_ATTN), row(N_HEADS), row(N_HEADS * AUG), row(N_HEADS * AUG),
                   pl.BlockSpec((1, D_ATTN, T), lambda i: (i, 0, 0))],
        out_shape=[jax.ShapeDtypeStruct((Ltot, D_ATTN), F32),
                   jax.ShapeDtypeStruct((Ltot, D_ATTN), F32),
                   jax.ShapeDtypeStruct((Ltot, N_HEADS), F32),
                   jax.ShapeDtypeStruct((Ltot, N_HEADS * AUG), BF16),
                   jax.ShapeDtypeStruct((Ltot, N_HEADS * AUG), BF16),
                   jax.ShapeDtypeStruct((Ltot // T, D_ATTN, T), BF16)],
        scratch_shapes=[pltpu.VMEM((1, N_HEADS), F32)],
        compiler_params=_params(1),
        name="kvq_proj_prompt",
    )(h2d, *consts)


def _proj_sample_kernel(h_ref, kvw_ref, nmw_ref, wk_ref, wv_ref, wf_ref, bf_ref, wq_ref,
                        k_ref, v_ref, logf_ref, q_ref):
    h = h_ref[...]
    hk = _rmsnorm(h, kvw_ref[...]).astype(BF16)
    u1 = _rmsnorm(h, nmw_ref[...]).astype(BF16)
    k_ref[...] = _dot(hk, wk_ref[...])
    v_ref[...] = _dot(hk, wv_ref[...])
    logf_ref[...] = _log_sigmoid(_dot(hk, wf_ref[...])[:, :N_HEADS] + bf_ref[...])
    q_ref[...] = _dot(u1, wq_ref[...]).astype(BF16)


def _proj_sample(h2d, w, *, T):
    n = h2d.shape[0]
    consts = [w["kv_norm"], w["mix_norm"], w["w_k"], w["w_v"], w["w_f"], w["b_f"], w["w_q"]]
    row = lambda m: pl.BlockSpec((T, m), lambda i: (i, 0))
    return pl.pallas_call(
        _proj_sample_kernel,
        grid=(n // T,),
        in_specs=[row(D_MODEL)] + [_const_spec(c.shape) for c in consts],
        out_specs=[row(D_ATTN), row(D_ATTN), row(N_HEADS), row(D_ATTN)],
        out_shape=[jax.ShapeDtypeStruct((n, D_ATTN), F32), jax.ShapeDtypeStruct((n, D_ATTN), F32),
                   jax.ShapeDtypeStruct((n, N_HEADS), F32), jax.ShapeDtypeStruct((n, D_ATTN), BF16)],
        compiler_params=_params(1),
        name="kvq_proj_sample",
    )(h2d, *consts)


def _attn_prompt_kernel(qa_ref, ka_ref, vT_ref, o_ref, acc_ref, *, T):
    qi = pl.program_id(1)
    q = [qa_ref[:, hh * AUG:(hh + 1) * AUG] for hh in range(2)]
    keep = _tri(T, False)

    def block(kb, carry, masked):
        out = []
        for hh in range(2):
            m, l, acc = carry[hh]
            sT = _dot_nt(ka_ref[pl.ds(pl.multiple_of(kb * T, T), T), hh * AUG:(hh + 1) * AUG], q[hh])
            if masked:
                sT = jnp.where(keep, sT, -jnp.inf)
            m_new = jnp.maximum(m, jnp.max(sT, axis=0, keepdims=True))
            alpha = jnp.exp(m - m_new)
            pT = jnp.exp(sT - m_new)
            l = alpha * l + jnp.sum(pT, axis=0, keepdims=True)
            vT = vT_ref[kb, hh * HEAD_DIM:(hh + 1) * HEAD_DIM, :]
            acc = alpha * acc + _dot(vT, pT.astype(BF16))
            out.append((m_new, l, acc))
        return tuple(out)

    init = tuple((jnp.full((1, T), -jnp.inf, F32), jnp.zeros((1, T), F32),
                  jnp.zeros((HEAD_DIM, T), F32)) for _ in range(2))
    carry = lax.fori_loop(0, qi, lambda kb, c: block(kb, c, False), init)
    carry = block(qi, carry, True)
    for hh in range(2):
        _, l, acc = carry[hh]
        acc_ref[hh * HEAD_DIM:(hh + 1) * HEAD_DIM, :] = acc * (1.0 / l)
    o_ref[...] = acc_ref[...].T.astype(BF16)


def _attn_prompt(qa, ka, vT, *, T):
    Ltot = qa.shape[0]
    nblk = Ltot // T
    return pl.pallas_call(
        functools.partial(_attn_prompt_kernel, T=T),
        grid=(N_HEADS // 2, nblk),
        in_specs=[pl.BlockSpec((T, 2 * AUG), lambda p, i: (i, p)),
                  pl.BlockSpec((Ltot, 2 * AUG), lambda p, i: (0, p)),
                  pl.BlockSpec((nblk, 2 * HEAD_DIM, T), lambda p, i: (0, p, 0))],
        out_specs=pl.BlockSpec((T, 2 * HEAD_DIM), lambda p, i: (i, p)),
        out_shape=jax.ShapeDtypeStruct((Ltot, D_ATTN), BF16),
        scratch_shapes=[pltpu.VMEM((2 * HEAD_DIM, T), F32)],
        compiler_params=_params(),
        name="fox_attn_prompt",
    )(qa, ka, vT)


def _attn_sample_kernel(q_ref, kn_ref, vn_ref, lfn_ref, ck_ref, cv_ref, lfT_ref, o_ref, *, l, P):
    HQ = 4
    R = HQ * l
    lane_p = lax.broadcasted_iota(jnp.int32, (N_HEADS, P), 1)
    suffix = lfT_ref[0]
    suffix = jnp.where(lane_p < P - 1, pltpu.roll(suffix, P - 1, axis=1), 0.0)
    d = 1
    while d < P:
        suffix = suffix + jnp.where(lane_p < P - d, pltpu.roll(suffix, P - d, axis=1), 0.0)
        d *= 2
    lfn = lfn_ref[0]
    cn = _sel_left(_tri(l, True).astype(BF16), lfn)
    cn_rows = jnp.concatenate([cn] * HQ, axis=0)
    row_head = lax.broadcasted_iota(jnp.int32, (R, N_HEADS), 0) // l
    lane_head = lax.broadcasted_iota(jnp.int32, (R, N_HEADS), 1)
    row_q = lax.broadcasted_iota(jnp.int32, (R, HQ * HEAD_DIM), 0) // l
    lane_q = lax.broadcasted_iota(jnp.int32, (R, HQ * HEAD_DIM), 1) // HEAD_DIM
    own = row_q == lane_q
    t_row = lax.broadcasted_iota(jnp.int32, (R, l), 0) % l
    j_col = lax.broadcasted_iota(jnp.int32, (R, l), 1)
    q = q_ref[0]
    outs = []
    for qd in range(N_HEADS // HQ):
        cols = slice(qd * HQ * HEAD_DIM, (qd + 1) * HQ * HEAD_DIM)
        sel = lane_head == row_head + qd * HQ
        qx = jnp.concatenate([q[:, cols]] * HQ, axis=0)
        qx = jnp.where(own, qx, jnp.zeros_like(qx))
        cn_col = jnp.sum(jnp.where(sel, cn_rows, 0.0), axis=1, keepdims=True)
        cn_keys = _sel_right_nt(sel.astype(BF16), cn)
        past_bias = jnp.concatenate(
            [jnp.broadcast_to(suffix[qd * HQ + j:qd * HQ + j + 1, :], (l, P)) for j in range(HQ)], axis=0)
        s_p = _dot_nt(qx, ck_ref[0, :, cols].astype(BF16)) + past_bias + cn_col
        s_n = _dot_nt(qx, kn_ref[0, :, cols].astype(BF16)) + cn_col - cn_keys
        s_n = jnp.where(j_col <= t_row, s_n, -jnp.inf)
        m = jnp.maximum(jnp.max(s_p, axis=1, keepdims=True), jnp.max(s_n, axis=1, keepdims=True))
        p_p = jnp.exp(s_p - m)
        p_n = jnp.exp(s_n - m)
        den = jnp.sum(p_p, axis=1, keepdims=True) + jnp.sum(p_n, axis=1, keepdims=True)
        o = (_dot(p_p.astype(BF16), cv_ref[0, :, cols].astype(BF16))
             + _dot(p_n.astype(BF16), vn_ref[0, :, cols].astype(BF16))) * (1.0 / den)
        o = jnp.where(own, o, 0.0)
        acc = o[0:l]
        for j in range(1, HQ):
            acc = acc + o[j * l:(j + 1) * l]
        outs.append(acc)
    o_ref[0] = jnp.concatenate(outs, axis=1).astype(BF16)


def _attn_sample(q, kn, vn, lfn, cache_k, cache_v, cache_lfT):
    B, l, _ = q.shape
    P = cache_k.shape[1]
    seq = lambda a, b: pl.BlockSpec((1, a, b), lambda i: (i, 0, 0))
    return pl.pallas_call(
        functools.partial(_attn_sample_kernel, l=l, P=P),
        grid=(B,),
        in_specs=[seq(l, D_ATTN), seq(l, D_ATTN), seq(l, D_ATTN), seq(l, N_HEADS),
                  seq(P, D_ATTN), seq(P, D_ATTN), seq(N_HEADS, P)],
        out_specs=seq(l, D_ATTN),
        out_shape=jax.ShapeDtypeStruct((B, l, D_ATTN), BF16),
        compiler_params=_params(1),
        name="fox_attn_sample",
    )(q, kn, vn, lfn, cache_k, cache_v, cache_lfT)


def _row(v):
    return v.reshape(1, -1).astype(F32)


def _col(v):
    return v.reshape(-1, 1).astype(F32)


def _prep_weights(norm_mix_w, norm_ffn_w, mamba_w_in, mamba_conv_w, mamba_conv_b, mamba_dt_bias,
                  mamba_A_log, mamba_D, mamba_norm_w, mamba_w_out, kv_norm_w, w_k, w_v, w_f, b_f,
                  attn_w_q, attn_w_o, ffn_w_up, ffn_conv_w, ffn_conv_b, ffn_w_down, final_norm_w):
    w_in = mamba_w_in[0]
    mamba = {
        "norm": _row(norm_mix_w[0]),
        "w_in": jnp.pad(w_in, ((0, 0), (0, IN_PROJ_PAD - IN_PROJ_DIM))).astype(BF16),
        "w_dtT": w_in[:, D_INNER + CONV_DIM:].T.astype(BF16),
        "conv_w": mamba_conv_w[0], "conv_b": _row(mamba_conv_b[0]),
        "dt_b": _row(mamba_dt_bias[0]), "dt_bT": _col(mamba_dt_bias[0]),
        "a_log": _row(mamba_A_log[0]), "a_logT": _col(mamba_A_log[0]),
        "d_x": _row(jnp.repeat(mamba_D[0], M_HEAD_DIM)),
        "gn_w": _row(mamba_norm_w[0]),
        "w_out": mamba_w_out[0].astype(BF16),
        "ex": jnp.repeat(jnp.eye(M_HEADS, dtype=BF16), M_HEAD_DIM, axis=1),
    }
    ffn = [{"norm": _row(norm_ffn_w[i]), "w_up": ffn_w_up[i].astype(BF16), "conv_w": ffn_conv_w[i],
            "conv_b": _row(ffn_conv_b[i]), "w_down": ffn_w_down[i].astype(BF16)} for i in range(2)]

    def per_head_pad(wm):
        wm = wm.reshape(D_MODEL, N_HEADS, HEAD_DIM)
        return jnp.pad(wm, ((0, 0), (0, 0), (0, AUG - HEAD_DIM))).reshape(D_MODEL, N_HEADS * AUG)

    def place(offset, value):
        m = jnp.zeros((3, N_HEADS, N_HEADS, AUG), F32)
        for i in range(3):
            m = m.at[i, :, :, offset + i].set(value * jnp.eye(N_HEADS, dtype=F32))
        return m.reshape(3, N_HEADS, N_HEADS * AUG).astype(BF16)

    def ones_at(offset):
        v = jnp.zeros((N_HEADS, AUG), F32).at[:, offset:offset + 3].set(1.0)
        return v.reshape(1, N_HEADS * AUG)

    w_q = attn_w_q[0] * ATTN_SCALE
    proj = {
        "kv_norm": _row(kv_norm_w), "mix_norm": _row(norm_mix_w[1]),
        "w_k": w_k.astype(BF16), "w_v": w_v.astype(BF16), "w_vT": w_v.T.astype(BF16),
        "w_f": jnp.pad(w_f, ((0, 0), (0, LANE - N_HEADS))).astype(BF16), "b_f": _row(b_f),
        "w_q": w_q.astype(BF16),
        "w_k_aug": per_head_pad(w_k).astype(BF16), "w_q_aug": per_head_pad(w_q).astype(BF16),
        "p_k": place(HEAD_DIM, -1.0), "p_q": place(HEAD_DIM + 3, 1.0),
        "one_k": ones_at(HEAD_DIM + 3), "one_q": ones_at(HEAD_DIM),
    }
    return mamba, ffn, proj, attn_w_o[0].astype(BF16), _row(final_norm_w)


PROMPT_TILE = 256
PROMPT_CHUNK = 128
ATTN_TILE = 256
SAMPLE_SEQS_MAMBA = 4
SAMPLE_SEQS_FFN = 16


def _trunk_prompt(x, weights):
    mamba, ffn, proj, w_o, final_w = weights
    B, Ltot, _ = x.shape
    T = min(PROMPT_TILE, Ltot)
    conv0 = jnp.zeros((B, M_CONV_W - 1, CONV_DIM), F32)
    ssm0 = jnp.zeros((B, M_HEADS, M_HEAD_DIM, D_STATE), F32)
    fconv0 = jnp.zeros((B, F_CONV_W - 1, 2 * D_FF), F32)
    h, mconv, ssm = _mamba_layer(x, conv0, ssm0, mamba, nb=1, L=T, Lc=min(PROMPT_CHUNK, T))
    h, fconv_a = _ffn_layer(h, fconv0, ffn[0], nb=1, L=T)
    TA = min(ATTN_TILE, Ltot)
    k, v, logf, qa, ka, vT = _proj_prompt(h.reshape(B * Ltot, D_MODEL), proj, T=TA)
    o = _attn_prompt(qa, ka, vT, T=TA)
    y, fconv_b = _ffn_layer(h, fconv0, ffn[1], nb=1, L=T, o=o.reshape(B, Ltot, D_ATTN), w_o=w_o,
                            final_w=final_w)
    return (y, ssm[None], mconv[None], jnp.stack([fconv_a, fconv_b]),
            k.reshape(B, Ltot, N_HEADS, HEAD_DIM), v.reshape(B, Ltot, N_HEADS, HEAD_DIM),
            logf.reshape(B, Ltot, N_HEADS))


def _trunk_sample(x, state_ssm, state_mconv, state_fconv, cache_k, cache_v, cache_logf, weights):
    mamba, ffn, proj, w_o, final_w = weights
    B, l, _ = x.shape
    P = cache_k.shape[1]
    h, mconv, ssm = _mamba_layer(x, state_mconv[0], state_ssm[0], mamba,
                                 nb=min(SAMPLE_SEQS_MAMBA, B), L=l, Lc=l)
    nbf = min(SAMPLE_SEQS_FFN, B)
    h, fconv_a = _ffn_layer(h, state_fconv[0], ffn[0], nb=nbf, L=l)
    k, v, logf, q = _proj_sample(h.reshape(B * l, D_MODEL), proj, T=nbf * l)
    o = _attn_sample(q.reshape(B, l, D_ATTN), k.reshape(B, l, D_ATTN), v.reshape(B, l, D_ATTN),
                     logf.reshape(B, l, N_HEADS), cache_k.reshape(B, P, D_ATTN),
                     cache_v.reshape(B, P, D_ATTN), jnp.swapaxes(cache_logf, 1, 2))
    y, fconv_b = _ffn_layer(h, state_fconv[1], ffn[1], nb=nbf, L=l, o=o, w_o=w_o, final_w=final_w)
    return (y, ssm[None], mconv[None], jnp.stack([fconv_a, fconv_b]),
            k.reshape(B, l, N_HEADS, HEAD_DIM), v.reshape(B, l, N_HEADS, HEAD_DIM),
            logf.reshape(B, l, N_HEADS))


def kernel(x_prompt, x_sample, state_ssm, state_mamba_conv, state_ffn_conv, cache_k, cache_v, cache_logf,
           norm_mix_w, norm_ffn_w, mamba_w_in, mamba_conv_w, mamba_conv_b, mamba_dt_bias, mamba_A_log,
           mamba_D, mamba_norm_w, mamba_w_out, kv_norm_w, w_k, w_v, w_f, b_f, attn_w_q, attn_w_o,
           ffn_w_up, ffn_conv_w, ffn_conv_b, ffn_w_down, final_norm_w):
    weights = _prep_weights(norm_mix_w, norm_ffn_w, mamba_w_in, mamba_conv_w, mamba_conv_b,
                            mamba_dt_bias, mamba_A_log, mamba_D, mamba_norm_w, mamba_w_out, kv_norm_w,
                            w_k, w_v, w_f, b_f, attn_w_q, attn_w_o, ffn_w_up, ffn_conv_w, ffn_conv_b,
                            ffn_w_down, final_norm_w)
    p = _trunk_prompt(x_prompt, weights)
    s = _trunk_sample(x_sample, state_ssm, state_mamba_conv, state_ffn_conv, cache_k, cache_v,
                      cache_logf, weights)
    return (p[0], s[0]) + p[1:] + s[1:]
```

```python
import functools

import jax
import jax.numpy as jnp
from jax import lax
from jax.experimental import pallas as pl
from jax.experimental.pallas import tpu as pltpu

F32 = jnp.float32
BF16 = jnp.bfloat16

EPS = 1e-6
D_MODEL = 1024
D_INNER = 2048
M_HEAD_DIM = 64
M_HEADS = 32
M_GROUPS = 4
HEADS_PER_GROUP = M_HEADS // M_GROUPS
D_STATE = 128
M_CONV_W = 4
CONV_DIM = D_INNER + 2 * M_GROUPS * D_STATE
IN_PROJ_DIM = D_INNER + CONV_DIM + M_HEADS
LANE = 128
IN_PROJ_PAD = -(-IN_PROJ_DIM // LANE) * LANE
HEAD_DIM = 64
N_HEADS = 16
D_ATTN = N_HEADS * HEAD_DIM
ATTN_SCALE = HEAD_DIM ** -0.5
LOG2E = 1.4426950408889634
D_FF = 2816
F_CONV_W = 3
HALO = 8
AUG = 2 * HEAD_DIM
VMEM_LIMIT = 56 * 1024 * 1024


def _dot(a, b):
    return jnp.dot(a, b, preferred_element_type=F32)


def _dot_nt(a, b):
    return lax.dot_general(a, b, (((1,), (1,)), ((), ())), preferred_element_type=F32)


def _dot_tn(a, b):
    return lax.dot_general(a, b, (((0,), (0,)), ((), ())), preferred_element_type=F32)


def _split3(x):
    hi = x.astype(BF16)
    r = x - hi.astype(F32)
    mid = r.astype(BF16)
    lo = (r - mid.astype(F32)).astype(BF16)
    return hi, mid, lo


def _sel_left(sel, x):
    hi, mid, lo = _split3(x)
    return _dot(sel, hi) + _dot(sel, mid) + _dot(sel, lo)


def _sel_right(x, sel):
    hi, mid, lo = _split3(x)
    return _dot(hi, sel) + _dot(mid, sel) + _dot(lo, sel)


def _sel_right_nt(sel, x):
    hi, mid, lo = _split3(x)
    return _dot_nt(sel, hi) + _dot_nt(sel, mid) + _dot_nt(sel, lo)


def _rmsnorm(x, w):
    return x * lax.rsqrt(jnp.mean(x * x, axis=-1, keepdims=True) + EPS) * w


def _silu(x):
    return x * (1.0 / (1.0 + jnp.exp(-x)))


def _softplus(x):
    return jnp.maximum(x, 0.0) + jnp.log1p(jnp.exp(-jnp.abs(x)))


def _tri(n, lower):
    r = lax.broadcasted_iota(jnp.int32, (n, n), 0)
    c = lax.broadcasted_iota(jnp.int32, (n, n), 1)
    return (r >= c) if lower else (r <= c)


def _const_spec(shape):
    nd = len(shape)
    return pl.BlockSpec(shape, lambda *_: (0,) * nd, pipeline_mode=pl.Buffered(1))


def _params(n_axes=2):
    return pltpu.CompilerParams(dimension_semantics=("arbitrary",) * n_axes,
                                vmem_limit_bytes=VMEM_LIMIT)


def _mamba_kernel(x_ref, conv0_ref, ssm0_ref, nw_ref, win_ref, wdtT_ref, cw_ref, cb_ref,
                  dtb_ref, dtbT_ref, alog_ref, alogT_ref, dx_ref, gnw_ref, wout_ref, ex_ref,
                  h_ref, convn_ref, ssmn_ref, fb_ref, y_ref, *, nb, L, Lc):
    M = nb * L
    lo = HALO - (M_CONV_W - 1)

    @pl.when(pl.program_id(1) == 0)
    def _():
        fb_ref[:, lo:HALO, :] = conv0_ref[...]
        ssmn_ref[...] = ssm0_ref[...]

    x = x_ref[...].reshape(M, D_MODEL)
    ub = _rmsnorm(x, nw_ref[...]).astype(BF16)
    zx = _dot(ub, win_ref[...])
    dtT = _softplus(_dot_nt(wdtT_ref[...], ub) + dtbT_ref[...])
    dt = _softplus(zx[:, D_INNER + CONV_DIM:IN_PROJ_DIM] + dtb_ref[...])

    fb_ref[:, HALO:HALO + L, :] = zx[:, D_INNER:D_INNER + CONV_DIM].reshape(nb, L, CONV_DIM)
    acc = cb_ref[...]
    for tap in range(M_CONV_W):
        acc = acc + fb_ref[:, lo + tap:lo + tap + L, :] * cw_ref[tap:tap + 1, :]
    halo = fb_ref[:, lo + L:HALO + L, :]
    fb_ref[:, lo:HALO, :] = halo
    convn_ref[...] = halo
    xbc = _silu(acc).reshape(M, CONV_DIM)
    xs = xbc[:, :D_INNER]
    bm = xbc[:, D_INNER:D_INNER + M_GROUPS * D_STATE].astype(BF16)
    cm = xbc[:, D_INNER + M_GROUPS * D_STATE:].astype(BF16)

    dA = dt * (-jnp.exp(alog_ref[...]))
    dAT = dtT * (-jnp.exp(alogT_ref[...]))
    tril = _tri(Lc, True)
    tril_b = tril.astype(BF16)
    triu_b = _tri(Lc, False).astype(BF16)
    ex = ex_ref[...]
    quad = lax.broadcasted_iota(jnp.int32, (Lc, 4 * M_HEAD_DIM), 1) // M_HEAD_DIM

    for s in range(nb):
        for c in range(L // Lc):
            r0 = s * L + c * Lc
            acum = _sel_left(tril_b, dA[r0:r0 + Lc])
            dtT_c = dtT[:, r0:r0 + Lc]
            acumT = _sel_right(dAT[:, r0:r0 + Lc], triu_b)
            acum_x = _sel_right(acum, ex)
            dt_x = _sel_right(dt[r0:r0 + Lc], ex)
            xs_c = xs[r0:r0 + Lc]
            xs_b = xs_c.astype(BF16)
            x_end = (xs_c * (jnp.exp(acum_x[Lc - 1:Lc, :] - acum_x) * dt_x)).astype(BF16)
            e_x = jnp.exp(acum_x)
            d_blk = jnp.exp(acumT[:, Lc - 1:Lc])
            for g in range(M_GROUPS):
                b_g = bm[r0:r0 + Lc, g * D_STATE:(g + 1) * D_STATE]
                c_g = cm[r0:r0 + Lc, g * D_STATE:(g + 1) * D_STATE]
                cb = _dot_nt(c_g, b_g)
                h0 = g * HEADS_PER_GROUP
                h_in = ssmn_ref[s, h0:h0 + HEADS_PER_GROUP].reshape(HEADS_PER_GROUP * M_HEAD_DIM, D_STATE)
                y_inter = _dot_nt(c_g, h_in.astype(BF16))
                for qd in range(HEADS_PER_GROUP // 4):
                    col = (h0 + 4 * qd) * M_HEAD_DIM
                    xq = xs_b[:, col:col + 4 * M_HEAD_DIM]
                    yq = jnp.zeros((Lc, 4 * M_HEAD_DIM), F32)
                    for j in range(4):
                        h = h0 + 4 * qd + j
                        seg = acum[:, h:h + 1] - acumT[h:h + 1, :]
                        w = cb * jnp.exp(jnp.where(tril, seg, -jnp.inf)) * dtT_c[h:h + 1, :]
                        yq = yq + _dot(w.astype(BF16), jnp.where(quad == j, xq, jnp.zeros_like(xq)))
                    y_ref[r0:r0 + Lc, col:col + 4 * M_HEAD_DIM] = (
                        yq + y_inter[:, qd * 4 * M_HEAD_DIM:(qd + 1) * 4 * M_HEAD_DIM]
                        * e_x[:, col:col + 4 * M_HEAD_DIM]
                        + dx_ref[:, col:col + 4 * M_HEAD_DIM] * xs_c[:, col:col + 4 * M_HEAD_DIM])
                st = _dot_tn(x_end[:, h0 * M_HEAD_DIM:(h0 + HEADS_PER_GROUP) * M_HEAD_DIM], b_g)
                for j in range(HEADS_PER_GROUP):
                    h = h0 + j
                    ssmn_ref[s, h] = (ssmn_ref[s, h] * d_blk[h:h + 1, :]
                                      + st[j * M_HEAD_DIM:(j + 1) * M_HEAD_DIM, :])

    yz = y_ref[...] * _silu(zx[:, :D_INNER])
    gs = D_INNER // M_GROUPS
    parts = []
    for g in range(M_GROUPS):
        blk = yz[:, g * gs:(g + 1) * gs]
        parts.append(_rmsnorm(blk, gnw_ref[:, g * gs:(g + 1) * gs]))
    yn = jnp.concatenate(parts, axis=1).astype(BF16)
    h_ref[...] = (x + _dot(yn, wout_ref[...])).reshape(nb, L, D_MODEL)


def _mamba_layer(x, conv0, ssm0, w, *, nb, L, Lc):
    B, Ltot, _ = x.shape
    grid = (B // nb, Ltot // L)
    consts = [w["norm"], w["w_in"], w["w_dtT"], w["conv_w"], w["conv_b"], w["dt_b"], w["dt_bT"],
              w["a_log"], w["a_logT"], w["d_x"], w["gn_w"], w["w_out"], w["ex"]]
    return pl.pallas_call(
        functools.partial(_mamba_kernel, nb=nb, L=L, Lc=Lc),
        grid=grid,
        in_specs=[pl.BlockSpec((nb, L, D_MODEL), lambda s, t: (s, t, 0)),
                  pl.BlockSpec((nb, M_CONV_W - 1, CONV_DIM), lambda s, t: (s, 0, 0)),
                  pl.BlockSpec((nb, M_HEADS, M_HEAD_DIM, D_STATE), lambda s, t: (s, 0, 0, 0))]
        + [_const_spec(c.shape) for c in consts],
        out_specs=[pl.BlockSpec((nb, L, D_MODEL), lambda s, t: (s, t, 0)),
                   pl.BlockSpec((nb, M_CONV_W - 1, CONV_DIM), lambda s, t: (s, 0, 0)),
                   pl.BlockSpec((nb, M_HEADS, M_HEAD_DIM, D_STATE), lambda s, t: (s, 0, 0, 0))],
        out_shape=[jax.ShapeDtypeStruct(x.shape, F32),
                   jax.ShapeDtypeStruct(conv0.shape, F32),
                   jax.ShapeDtypeStruct(ssm0.shape, F32)],
        scratch_shapes=[pltpu.VMEM((nb, HALO + L, CONV_DIM), F32),
                        pltpu.VMEM((nb * L, D_INNER), F32)],
        compiler_params=_params(),
        name="mamba_layer",
    )(x, conv0, ssm0, *consts)


def _ffn_kernel(*refs, nb, L, with_attn, final):
    it = iter(refs)
    h_ref = next(it)
    if with_attn:
        o_ref, wo_ref = next(it), next(it)
    fconv0_ref, nw_ref, wup_ref, cw_ref, cb_ref, wdown_ref = (next(it) for _ in range(6))
    if final:
        fnw_ref = next(it)
    out_ref, fconvn_ref, fb_ref = next(it), next(it), next(it)
    M = nb * L
    lo = HALO - (F_CONV_W - 1)

    @pl.when(pl.program_id(1) == 0)
    def _():
        fb_ref[:, lo:HALO, :] = fconv0_ref[...]

    h = h_ref[...].reshape(M, D_MODEL)
    if with_attn:
        h = h + _dot(o_ref[...].reshape(M, D_ATTN), wo_ref[...])
    ub = _rmsnorm(h, nw_ref[...]).astype(BF16)
    fb_ref[:, HALO:HALO + L, :] = _dot(ub, wup_ref[...]).reshape(nb, L, 2 * D_FF)
    acc = cb_ref[...]
    for tap in range(F_CONV_W):
        acc = acc + fb_ref[:, lo + tap:lo + tap + L, :] * cw_ref[tap:tap + 1, :]
    halo = fb_ref[:, lo + L:HALO + L, :]
    fb_ref[:, lo:HALO, :] = halo
    fconvn_ref[...] = halo
    a = acc.reshape(M, 2 * D_FF)
    act = (_silu(a[:, :D_FF]) * a[:, D_FF:]).astype(BF16)
    hn = h + _dot(act, wdown_ref[...])
    if final:
        hn = _rmsnorm(hn, fnw_ref[...])
    out_ref[...] = hn.reshape(nb, L, D_MODEL)


def _ffn_layer(h, fconv0, w, *, nb, L, o=None, w_o=None, final_w=None):
    B, Ltot, _ = h.shape
    grid = (B // nb, Ltot // L)
    with_attn, final = o is not None, final_w is not None
    tile = pl.BlockSpec((nb, L, D_MODEL), lambda s, t: (s, t, 0))
    args, specs = [h], [tile]
    if with_attn:
        args += [o, w_o]
        specs += [pl.BlockSpec((nb, L, D_ATTN), lambda s, t: (s, t, 0)), _const_spec(w_o.shape)]
    consts = [w["norm"], w["w_up"], w["conv_w"], w["conv_b"], w["w_down"]]
    args += [fconv0] + consts
    specs += [pl.BlockSpec((nb, F_CONV_W - 1, 2 * D_FF), lambda s, t: (s, 0, 0))]
    specs += [_const_spec(c.shape) for c in consts]
    if final:
        args.append(final_w)
        specs.append(_const_spec(final_w.shape))
    return pl.pallas_call(
        functools.partial(_ffn_kernel, nb=nb, L=L, with_attn=with_attn, final=final),
        grid=grid,
        in_specs=specs,
        out_specs=[tile, pl.BlockSpec((nb, F_CONV_W - 1, 2 * D_FF), lambda s, t: (s, 0, 0))],
        out_shape=[jax.ShapeDtypeStruct(h.shape, F32), jax.ShapeDtypeStruct(fconv0.shape, F32)],
        scratch_shapes=[pltpu.VMEM((nb, HALO + L, 2 * D_FF), F32)],
        compiler_params=_params(),
        name="conv_ffn",
    )(*args)


def _log_sigmoid(x):
    return -_softplus(-x)


def _proj_prompt_kernel(h_ref, kvw_ref, nmw_ref, wk_ref, wv_ref, wvT_ref, wf_ref, bf_ref,
                        wkp_ref, wqp_ref, pk_ref, pq_ref, onek_ref, oneq_ref,
                        k_ref, v_ref, logf_ref, qa_ref, ka_ref, vT_ref, carry_ref, *, T):
    @pl.when(pl.program_id(0) == 0)
    def _():
        carry_ref[...] = jnp.zeros_like(carry_ref)

    h = h_ref[...]
    hk = _rmsnorm(h, kvw_ref[...]).astype(BF16)
    u1 = _rmsnorm(h, nmw_ref[...]).astype(BF16)
    k_ref[...] = _dot(hk, wk_ref[...])
    v_ref[...] = _dot(hk, wv_ref[...])
    vT_ref[0] = _dot_nt(wvT_ref[...], hk).astype(BF16)
    logf = _log_sigmoid(_dot(hk, wf_ref[...])[:, :N_HEADS] + bf_ref[...])
    logf_ref[...] = logf
    c = carry_ref[...] + _sel_left(_tri(T, True).astype(BF16), logf)
    carry_ref[...] = c[T - 1:T, :]
    c_hi, c_mid, c_lo = _split3(c * LOG2E)
    ka = _dot(hk, wkp_ref[...]) + onek_ref[...]
    qa = _dot(u1, wqp_ref[...]) + oneq_ref[...]
    for i, part in enumerate((c_hi, c_mid, c_lo)):
        ka = ka + _dot(part, pk_ref[i])
        qa = qa + _dot(part, pq_ref[i])
    ka_ref[...] = ka.astype(BF16)
    qa_ref[...] = qa.astype(BF16)


def _proj_prompt(h2d, w, *, T):
    Ltot = h2d.shape[0]
    consts = [w["kv_norm"], w["mix_norm"], w["w_k"], w["w_v"], w["w_vT"], w["w_f"], w["b_f"],
              w["w_k_aug"], w["w_q_aug"], w["p_k"], w["p_q"], w["one_k"], w["one_q"]]
    row = lambda n: pl.BlockSpec((T, n), lambda i: (i, 0))
    return pl.pallas_call(
        functools.partial(_proj_prompt_kernel, T=T),
        grid=(Ltot // T,),
        in_specs=[row(D_MODEL)] + [_const_spec(c.shape) for c in consts],
        out_specs=[row(D_ATTN), row(D_ATTN), row(N_HEADS), row(N_HEADS * AUG), row(N_HEADS * AUG),
                   pl.BlockSpec((1, D_ATTN, T), lambda i: (i, 0, 0))],
        out_shape=[jax.ShapeDtypeStruct((Ltot, D_ATTN), F32),
                   jax.ShapeDtypeStruct((Ltot, D_ATTN), F32),
                   jax.ShapeDtypeStruct((Ltot, N_HEADS), F32),
                   jax.ShapeDtypeStruct((Ltot, N_HEADS * AUG), BF16),
                   jax.ShapeDtypeStruct((Ltot, N_HEADS * AUG), BF16),
                   jax.ShapeDtypeStruct((Ltot // T, D_ATTN, T), BF16)],
        scratch_shapes=[pltpu.VMEM((1, N_HEADS), F32)],
        compiler_params=_params(1),
        name="kvq_proj_prompt",
    )(h2d, *consts)


def _proj_sample_kernel(h_ref, kvw_ref, nmw_ref, wk_ref, wv_ref, wf_ref, bf_ref, wq_ref,
                        k_ref, v_ref, logf_ref, q_ref):
    h = h_ref[...]
    hk = _rmsnorm(h, kvw_ref[...]).astype(BF16)
    u1 = _rmsnorm(h, nmw_ref[...]).astype(BF16)
    k_ref[...] = _dot(hk, wk_ref[...])
    v_ref[...] = _dot(hk, wv_ref[...])
    logf_ref[...] = _log_sigmoid(_dot(hk, wf_ref[...])[:, :N_HEADS] + bf_ref[...])
    q_ref[...] = _dot(u1, wq_ref[...]).astype(BF16)


def _proj_sample(h2d, w, *, T):
    n = h2d.shape[0]
    consts = [w["kv_norm"], w["mix_norm"], w["w_k"], w["w_v"], w["w_f"], w["b_f"], w["w_q"]]
    row = lambda m: pl.BlockSpec((T, m), lambda i: (i, 0))
    return pl.pallas_call(
        _proj_sample_kernel,
        grid=(n // T,),
        in_specs=[row(D_MODEL)] + [_const_spec(c.shape) for c in consts],
        out_specs=[row(D_ATTN), row(D_ATTN), row(N_HEADS), row(D_ATTN)],
        out_shape=[jax.ShapeDtypeStruct((n, D_ATTN), F32), jax.ShapeDtypeStruct((n, D_ATTN), F32),
                   jax.ShapeDtypeStruct((n, N_HEADS), F32), jax.ShapeDtypeStruct((n, D_ATTN), BF16)],
        compiler_params=_params(1),
        name="kvq_proj_sample",
    )(h2d, *consts)


def _attn_prompt_kernel(qa_ref, ka_ref, vT_ref, o_ref, s_ref, acc_ref, *, T, nblk):
    qi = pl.program_id(1)
    q = [qa_ref[:, hh * AUG:(hh + 1) * AUG] for hh in range(2)]
    key_minus_query = (lax.broadcasted_iota(jnp.int32, (T, T), 0)
                       - lax.broadcasted_iota(jnp.int32, (T, T), 1))

    def produce(kb, slot):
        start = pl.multiple_of(jnp.minimum(kb, nblk - 1) * T, T)
        for hh in range(2):
            s_ref[slot, hh] = _dot_nt(ka_ref[pl.ds(start, T), hh * AUG:(hh + 1) * AUG], q[hh])

    def consume(kb, slot, carry, masked):
        out = []
        for hh in range(2):
            m, l, acc = carry[hh]
            sT = s_ref[slot, hh]
            if masked:
                sT = jnp.where(key_minus_query <= (qi - kb) * T, sT, -jnp.inf)
            m_new = jnp.maximum(m, jnp.max(sT, axis=0, keepdims=True))
            alpha = jnp.exp2(m - m_new)
            pT = jnp.exp2(sT - m_new)
            l = alpha * l + jnp.sum(pT, axis=0, keepdims=True)
            vT = vT_ref[jnp.minimum(kb, nblk - 1), hh * HEAD_DIM:(hh + 1) * HEAD_DIM, :]
            acc = alpha * acc + _dot(vT, pT.astype(BF16))
            out.append((m_new, l, acc))
        return tuple(out)

    def step(kb, slot, carry, masked):
        produce(kb + 1, 1 - slot)
        return consume(kb, slot, carry, masked)

    def pair(i, carry):
        return step(2 * i + 1, 1, step(2 * i, 0, carry, False), False)

    init = tuple((jnp.full((1, T), -jnp.inf, F32), jnp.zeros((1, T), F32),
                  jnp.zeros((HEAD_DIM, T), F32)) for _ in range(2))
    produce(0, 0)
    carry = lax.fori_loop(0, qi // 2, pair, init)
    last = 2 * (qi // 2)
    carry = step(last, 0, carry, True)
    carry = consume(last + 1, 1, carry, True)
    for hh in range(2):
        _, l, acc = carry[hh]
        acc_ref[hh * HEAD_DIM:(hh + 1) * HEAD_DIM, :] = acc * (1.0 / l)
    o_ref[...] = acc_ref[...].T.astype(BF16)


def _attn_prompt(qa, ka, vT, *, T):
    Ltot = qa.shape[0]
    nblk = Ltot // T
    return pl.pallas_call(
        functools.partial(_attn_prompt_kernel, T=T, nblk=nblk),
        grid=(N_HEADS // 2, nblk),
        in_specs=[pl.BlockSpec((T, 2 * AUG), lambda p, i: (i, p)),
                  pl.BlockSpec((Ltot, 2 * AUG), lambda p, i: (0, p)),
                  pl.BlockSpec((nblk, 2 * HEAD_DIM, T), lambda p, i: (0, p, 0))],
        out_specs=pl.BlockSpec((T, 2 * HEAD_DIM), lambda p, i: (i, p)),
        out_shape=jax.ShapeDtypeStruct((Ltot, D_ATTN), BF16),
        scratch_shapes=[pltpu.VMEM((2, 2, T, T), F32), pltpu.VMEM((2 * HEAD_DIM, T), F32)],
        compiler_params=_params(),
        name="fox_attn_prompt",
    )(qa, ka, vT)


def _attn_sample_kernel(q_ref, kn_ref, vn_ref, lfn_ref, ck_ref, cv_ref, lfT_ref, o_ref, *, l, P):
    HQ = 4
    R = HQ * l
    lane_p = lax.broadcasted_iota(jnp.int32, (N_HEADS, P), 1)
    suffix = lfT_ref[0]
    suffix = jnp.where(lane_p < P - 1, pltpu.roll(suffix, P - 1, axis=1), 0.0)
    d = 1
    while d < P:
        suffix = suffix + jnp.where(lane_p < P - d, pltpu.roll(suffix, P - d, axis=1), 0.0)
        d *= 2
    lfn = lfn_ref[0]
    cn = _sel_left(_tri(l, True).astype(BF16), lfn)
    cn_rows = jnp.concatenate([cn] * HQ, axis=0)
    row_head = lax.broadcasted_iota(jnp.int32, (R, N_HEADS), 0) // l
    lane_head = lax.broadcasted_iota(jnp.int32, (R, N_HEADS), 1)
    row_q = lax.broadcasted_iota(jnp.int32, (R, HQ * HEAD_DIM), 0) // l
    lane_q = lax.broadcasted_iota(jnp.int32, (R, HQ * HEAD_DIM), 1) // HEAD_DIM
    own = row_q == lane_q
    t_row = lax.broadcasted_iota(jnp.int32, (R, l), 0) % l
    j_col = lax.broadcasted_iota(jnp.int32, (R, l), 1)
    q = q_ref[0]
    outs = []
    for qd in range(N_HEADS // HQ):
        cols = slice(qd * HQ * HEAD_DIM, (qd + 1) * HQ * HEAD_DIM)
        sel = lane_head == row_head + qd * HQ
        qx = jnp.concatenate([q[:, cols]] * HQ, axis=0)
        qx = jnp.where(own, qx, jnp.zeros_like(qx))
        cn_col = jnp.sum(jnp.where(sel, cn_rows, 0.0), axis=1, keepdims=True)
        cn_keys = _sel_right_nt(sel.astype(BF16), cn)
        past_bias = jnp.concatenate(
            [jnp.broadcast_to(suffix[qd * HQ + j:qd * HQ + j + 1, :], (l, P)) for j in range(HQ)], axis=0)
        s_p = _dot_nt(qx, ck_ref[0, :, cols].astype(BF16)) + past_bias + cn_col
        s_n = _dot_nt(qx, kn_ref[0, :, cols].astype(BF16)) + cn_col - cn_keys
        s_n = jnp.where(j_col <= t_row, s_n, -jnp.inf)
        m = jnp.maximum(jnp.max(s_p, axis=1, keepdims=True), jnp.max(s_n, axis=1, keepdims=True))
        p_p = jnp.exp(s_p - m)
        p_n = jnp.exp(s_n - m)
        den = jnp.sum(p_p, axis=1, keepdims=True) + jnp.sum(p_n, axis=1, keepdims=True)
        o = (_dot(p_p.astype(BF16), cv_ref[0, :, cols].astype(BF16))
             + _dot(p_n.astype(BF16), vn_ref[0, :, cols].astype(BF16))) * (1.0 / den)
        o = jnp.where(own, o, 0.0)
        acc = o[0:l]
        for j in range(1, HQ):
            acc = acc + o[j * l:(j + 1) * l]
        outs.append(acc)
    o_ref[0] = jnp.concatenate(outs, axis=1).astype(BF16)


def _attn_sample(q, kn, vn, lfn, cache_k, cache_v, cache_lfT):
    B, l, _ = q.shape
    P = cache_k.shape[1]
    seq = lambda a, b: pl.BlockSpec((1, a, b), lambda i: (i, 0, 0))
    return pl.pallas_call(
        functools.partial(_attn_sample_kernel, l=l, P=P),
        grid=(B,),
        in_specs=[seq(l, D_ATTN), seq(l, D_ATTN), seq(l, D_ATTN), seq(l, N_HEADS),
                  seq(P, D_ATTN), seq(P, D_ATTN), seq(N_HEADS, P)],
        out_specs=seq(l, D_ATTN),
        out_shape=jax.ShapeDtypeStruct((B, l, D_ATTN), BF16),
        compiler_params=_params(1),
        name="fox_attn_sample",
    )(q, kn, vn, lfn, cache_k, cache_v, cache_lfT)


def _row(v):
    return v.reshape(1, -1).astype(F32)


def _col(v):
    return v.reshape(-1, 1).astype(F32)


def _prep_weights(norm_mix_w, norm_ffn_w, mamba_w_in, mamba_conv_w, mamba_conv_b, mamba_dt_bias,
                  mamba_A_log, mamba_D, mamba_norm_w, mamba_w_out, kv_norm_w, w_k, w_v, w_f, b_f,
                  attn_w_q, attn_w_o, ffn_w_up, ffn_conv_w, ffn_conv_b, ffn_w_down, final_norm_w):
    w_in = mamba_w_in[0]
    mamba = {
        "norm": _row(norm_mix_w[0]),
        "w_in": jnp.pad(w_in, ((0, 0), (0, IN_PROJ_PAD - IN_PROJ_DIM))).astype(BF16),
        "w_dtT": w_in[:, D_INNER + CONV_DIM:].T.astype(BF16),
        "conv_w": mamba_conv_w[0], "conv_b": _row(mamba_conv_b[0]),
        "dt_b": _row(mamba_dt_bias[0]), "dt_bT": _col(mamba_dt_bias[0]),
        "a_log": _row(mamba_A_log[0]), "a_logT": _col(mamba_A_log[0]),
        "d_x": _row(jnp.repeat(mamba_D[0], M_HEAD_DIM)),
        "gn_w": _row(mamba_norm_w[0]),
        "w_out": mamba_w_out[0].astype(BF16),
        "ex": jnp.repeat(jnp.eye(M_HEADS, dtype=BF16), M_HEAD_DIM, axis=1),
    }
    ffn = [{"norm": _row(norm_ffn_w[i]), "w_up": ffn_w_up[i].astype(BF16), "conv_w": ffn_conv_w[i],
            "conv_b": _row(ffn_conv_b[i]), "w_down": ffn_w_down[i].astype(BF16)} for i in range(2)]

    def per_head_pad(wm):
        wm = wm.reshape(D_MODEL, N_HEADS, HEAD_DIM)
        return jnp.pad(wm, ((0, 0), (0, 0), (0, AUG - HEAD_DIM))).reshape(D_MODEL, N_HEADS * AUG)

    def place(offset, value):
        m = jnp.zeros((3, N_HEADS, N_HEADS, AUG), F32)
        for i in range(3):
            m = m.at[i, :, :, offset + i].set(value * jnp.eye(N_HEADS, dtype=F32))
        return m.reshape(3, N_HEADS, N_HEADS * AUG).astype(BF16)

    def ones_at(offset):
        v = jnp.zeros((N_HEADS, AUG), F32).at[:, offset:offset + 3].set(1.0)
        return v.reshape(1, N_HEADS * AUG)

    w_q = attn_w_q[0] * ATTN_SCALE
    proj = {
        "kv_norm": _row(kv_norm_w), "mix_norm": _row(norm_mix_w[1]),
        "w_k": w_k.astype(BF16), "w_v": w_v.astype(BF16), "w_vT": w_v.T.astype(BF16),
        "w_f": jnp.pad(w_f, ((0, 0), (0, LANE - N_HEADS))).astype(BF16), "b_f": _row(b_f),
        "w_q": w_q.astype(BF16),
        "w_k_aug": per_head_pad(w_k).astype(BF16), "w_q_aug": per_head_pad(w_q * LOG2E).astype(BF16),
        "p_k": place(HEAD_DIM, -1.0), "p_q": place(HEAD_DIM + 3, 1.0),
        "one_k": ones_at(HEAD_DIM + 3), "one_q": ones_at(HEAD_DIM),
    }
    return mamba, ffn, proj, attn_w_o[0].astype(BF16), _row(final_norm_w)


PROMPT_TILE = 256
PROMPT_CHUNK = 128
ATTN_TILE = 256
SAMPLE_SEQS_MAMBA = 4
SAMPLE_SEQS_FFN = 16


def _trunk_prompt(x, weights):
    mamba, ffn, proj, w_o, final_w = weights
    B, Ltot, _ = x.shape
    T = min(PROMPT_TILE, Ltot)
    conv0 = jnp.zeros((B, M_CONV_W - 1, CONV_DIM), F32)
    ssm0 = jnp.zeros((B, M_HEADS, M_HEAD_DIM, D_STATE), F32)
    fconv0 = jnp.zeros((B, F_CONV_W - 1, 2 * D_FF), F32)
    h, mconv, ssm = _mamba_layer(x, conv0, ssm0, mamba, nb=1, L=T, Lc=min(PROMPT_CHUNK, T))
    h, fconv_a = _ffn_layer(h, fconv0, ffn[0], nb=1, L=T)
    TA = min(ATTN_TILE, Ltot)
    k, v, logf, qa, ka, vT = _proj_prompt(h.reshape(B * Ltot, D_MODEL), proj, T=TA)
    o = _attn_prompt(qa, ka, vT, T=TA)
    y, fconv_b = _ffn_layer(h, fconv0, ffn[1], nb=1, L=T, o=o.reshape(B, Ltot, D_ATTN), w_o=w_o,
                            final_w=final_w)
    return (y, ssm[None], mconv[None], jnp.stack([fconv_a, fconv_b]),
            k.reshape(B, Ltot, N_HEADS, HEAD_DIM), v.reshape(B, Ltot, N_HEADS, HEAD_DIM),
            logf.reshape(B, Ltot, N_HEADS))


def _trunk_sample(x, state_ssm, state_mconv, state_fconv, cache_k, cache_v, cache_logf, weights):
    mamba, ffn, proj, w_o, final_w = weights
    B, l, _ = x.shape
    P = cache_k.shape[1]
    h, mconv, ssm = _mamba_layer(x, state_mconv[0], state_ssm[0], mamba,
                                 nb=min(SAMPLE_SEQS_MAMBA, B), L=l, Lc=l)
    nbf = min(SAMPLE_SEQS_FFN, B)
    h, fconv_a = _ffn_layer(h, state_fconv[0], ffn[0], nb=nbf, L=l)
    k, v, logf, q = _proj_sample(h.reshape(B * l, D_MODEL), proj, T=nbf * l)
    o = _attn_sample(q.reshape(B, l, D_ATTN), k.reshape(B, l, D_ATTN), v.reshape(B, l, D_ATTN),
                     logf.reshape(B, l, N_HEADS), cache_k.reshape(B, P, D_ATTN),
                     cache_v.reshape(B, P, D_ATTN), jnp.swapaxes(cache_logf, 1, 2))
    y, fconv_b = _ffn_layer(h, state_fconv[1], ffn[1], nb=nbf, L=l, o=o, w_o=w_o, final_w=final_w)
    return (y, ssm[None], mconv[None], jnp.stack([fconv_a, fconv_b]),
            k.reshape(B, l, N_HEADS, HEAD_DIM), v.reshape(B, l, N_HEADS, HEAD_DIM),
            logf.reshape(B, l, N_HEADS))


def kernel(x_prompt, x_sample, state_ssm, state_mamba_conv, state_ffn_conv, cache_k, cache_v, cache_logf,
           norm_mix_w, norm_ffn_w, mamba_w_in, mamba_conv_w, mamba_conv_b, mamba_dt_bias, mamba_A_log,
           mamba_D, mamba_norm_w, mamba_w_out, kv_norm_w, w_k, w_v, w_f, b_f, attn_w_q, attn_w_o,
           ffn_w_up, ffn_conv_w, ffn_conv_b, ffn_w_down, final_norm_w):
    weights = _prep_weights(norm_mix_w, norm_ffn_w, mamba_w_in, mamba_conv_w, mamba_conv_b,
                            mamba_dt_bias, mamba_A_log, mamba_D, mamba_norm_w, mamba_w_out, kv_norm_w,
                            w_k, w_v, w_f, b_f, attn_w_q, attn_w_o, ffn_w_up, ffn_conv_w, ffn_conv_b,
                            ffn_w_down, final_norm_w)
    p = _trunk_prompt(x_prompt, weights)
    s = _trunk_sample(x_sample, state_ssm, state_mamba_conv, state_ffn_conv, cache_k, cache_v,
                      cache_logf, weights)
    return (p[0], s[0]) + p[1:] + s[1:]
```

```python
import functools

import jax
import jax.numpy as jnp
from jax import lax
from jax.experimental import pallas as pl
from jax.experimental.pallas import tpu as pltpu

F32 = jnp.float32
BF16 = jnp.bfloat16

EPS = 1e-6
D_MODEL = 1024
D_INNER = 2048
M_HEAD_DIM = 64
M_HEADS = 32
M_GROUPS = 4
HEADS_PER_GROUP = M_HEADS // M_GROUPS
D_STATE = 128
M_CONV_W = 4
CONV_DIM = D_INNER + 2 * M_GROUPS * D_STATE
IN_PROJ_DIM = D_INNER + CONV_DIM + M_HEADS
LANE = 128
IN_PROJ_PAD = -(-IN_PROJ_DIM // LANE) * LANE
HEAD_DIM = 64
N_HEADS = 16
D_ATTN = N_HEADS * HEAD_DIM
ATTN_SCALE = HEAD_DIM ** -0.5
LOG2E = 1.4426950408889634
D_FF = 2816
F_CONV_W = 3
HALO = 8
AUG = 2 * HEAD_DIM
VMEM_LIMIT = 56 * 1024 * 1024


def _dot(a, b):
    return jnp.dot(a, b, preferred_element_type=F32)


def _dot_nt(a, b):
    return lax.dot_general(a, b, (((1,), (1,)), ((), ())), preferred_element_type=F32)


def _dot_tn(a, b):
    return lax.dot_general(a, b, (((0,), (0,)), ((), ())), preferred_element_type=F32)


def _split3(x):
    hi = x.astype(BF16)
    r = x - hi.astype(F32)
    mid = r.astype(BF16)
    lo = (r - mid.astype(F32)).astype(BF16)
    return hi, mid, lo


def _sel_left(sel, x):
    hi, mid, lo = _split3(x)
    return _dot(sel, hi) + _dot(sel, mid) + _dot(sel, lo)


def _sel_right(x, sel):
    hi, mid, lo = _split3(x)
    return _dot(hi, sel) + _dot(mid, sel) + _dot(lo, sel)


def _sel_right_nt(sel, x):
    hi, mid, lo = _split3(x)
    return _dot_nt(sel, hi) + _dot_nt(sel, mid) + _dot_nt(sel, lo)


def _rmsnorm(x, w):
    return x * lax.rsqrt(jnp.mean(x * x, axis=-1, keepdims=True) + EPS) * w


def _silu(x):
    return x * (1.0 / (1.0 + jnp.exp(-x)))


def _softplus(x):
    return jnp.maximum(x, 0.0) + jnp.log1p(jnp.exp(-jnp.abs(x)))


def _tri(n, lower):
    r = lax.broadcasted_iota(jnp.int32, (n, n), 0)
    c = lax.broadcasted_iota(jnp.int32, (n, n), 1)
    return (r >= c) if lower else (r <= c)


def _const_spec(shape):
    nd = len(shape)
    return pl.BlockSpec(shape, lambda *_: (0,) * nd, pipeline_mode=pl.Buffered(1))


def _params(n_axes=2):
    return pltpu.CompilerParams(dimension_semantics=("arbitrary",) * n_axes,
                                vmem_limit_bytes=VMEM_LIMIT)


def _mamba_kernel(x_ref, conv0_ref, ssm0_ref, nw_ref, win_ref, wdtT_ref, cw_ref, cb_ref,
                  dtb_ref, dtbT_ref, alog_ref, alogT_ref, dx_ref, gnw_ref, wout_ref, ex_ref,
                  h_ref, convn_ref, ssmn_ref, fb_ref, y_ref, *, nb, L, Lc):
    M = nb * L
    lo = HALO - (M_CONV_W - 1)

    @pl.when(pl.program_id(1) == 0)
    def _():
        fb_ref[:, lo:HALO, :] = conv0_ref[...]
        ssmn_ref[...] = ssm0_ref[...]

    x = x_ref[...].reshape(M, D_MODEL)
    ub = _rmsnorm(x, nw_ref[...]).astype(BF16)
    zx = _dot(ub, win_ref[...])
    dtT = _softplus(_dot_nt(wdtT_ref[...], ub) + dtbT_ref[...])
    dt = _softplus(zx[:, D_INNER + CONV_DIM:IN_PROJ_DIM] + dtb_ref[...])

    fb_ref[:, HALO:HALO + L, :] = zx[:, D_INNER:D_INNER + CONV_DIM].reshape(nb, L, CONV_DIM)
    acc = cb_ref[...]
    for tap in range(M_CONV_W):
        acc = acc + fb_ref[:, lo + tap:lo + tap + L, :] * cw_ref[tap:tap + 1, :]
    halo = fb_ref[:, lo + L:HALO + L, :]
    fb_ref[:, lo:HALO, :] = halo
    convn_ref[...] = halo
    xbc = _silu(acc).reshape(M, CONV_DIM)
    xs = xbc[:, :D_INNER]
    bm = xbc[:, D_INNER:D_INNER + M_GROUPS * D_STATE].astype(BF16)
    cm = xbc[:, D_INNER + M_GROUPS * D_STATE:].astype(BF16)

    dA = dt * (-jnp.exp(alog_ref[...]))
    dAT = dtT * (-jnp.exp(alogT_ref[...]))
    tril = _tri(Lc, True)
    tril_b = tril.astype(BF16)
    triu_b = _tri(Lc, False).astype(BF16)
    ex = ex_ref[...]
    quad = lax.broadcasted_iota(jnp.int32, (Lc, 4 * M_HEAD_DIM), 1) // M_HEAD_DIM

    for s in range(nb):
        for c in range(L // Lc):
            r0 = s * L + c * Lc
            acum = _sel_left(tril_b, dA[r0:r0 + Lc])
            dtT_c = dtT[:, r0:r0 + Lc]
            acumT = _sel_right(dAT[:, r0:r0 + Lc], triu_b)
            acum_x = _sel_right(acum, ex)
            dt_x = _sel_right(dt[r0:r0 + Lc], ex)
            xs_c = xs[r0:r0 + Lc]
            xs_b = xs_c.astype(BF16)
            x_end = (xs_c * (jnp.exp(acum_x[Lc - 1:Lc, :] - acum_x) * dt_x)).astype(BF16)
            e_x = jnp.exp(acum_x)
            d_blk = jnp.exp(acumT[:, Lc - 1:Lc])
            for g in range(M_GROUPS):
                b_g = bm[r0:r0 + Lc, g * D_STATE:(g + 1) * D_STATE]
                c_g = cm[r0:r0 + Lc, g * D_STATE:(g + 1) * D_STATE]
                cb = _dot_nt(c_g, b_g)
                h0 = g * HEADS_PER_GROUP
                h_in = ssmn_ref[s, h0:h0 + HEADS_PER_GROUP].reshape(HEADS_PER_GROUP * M_HEAD_DIM, D_STATE)
                y_inter = _dot_nt(c_g, h_in.astype(BF16))
                for qd in range(HEADS_PER_GROUP // 4):
                    col = (h0 + 4 * qd) * M_HEAD_DIM
                    xq = xs_b[:, col:col + 4 * M_HEAD_DIM]
                    yq = jnp.zeros((Lc, 4 * M_HEAD_DIM), F32)
                    for j in range(4):
                        h = h0 + 4 * qd + j
                        seg = acum[:, h:h + 1] - acumT[h:h + 1, :]
                        w = cb * jnp.exp(jnp.where(tril, seg, -jnp.inf)) * dtT_c[h:h + 1, :]
                        yq = yq + _dot(w.astype(BF16), jnp.where(quad == j, xq, jnp.zeros_like(xq)))
                    y_ref[r0:r0 + Lc, col:col + 4 * M_HEAD_DIM] = (
                        yq + y_inter[:, qd * 4 * M_HEAD_DIM:(qd + 1) * 4 * M_HEAD_DIM]
                        * e_x[:, col:col + 4 * M_HEAD_DIM]
                        + dx_ref[:, col:col + 4 * M_HEAD_DIM] * xs_c[:, col:col + 4 * M_HEAD_DIM])
                st = _dot_tn(x_end[:, h0 * M_HEAD_DIM:(h0 + HEADS_PER_GROUP) * M_HEAD_DIM], b_g)
                for j in range(HEADS_PER_GROUP):
                    h = h0 + j
                    ssmn_ref[s, h] = (ssmn_ref[s, h] * d_blk[h:h + 1, :]
                                      + st[j * M_HEAD_DIM:(j + 1) * M_HEAD_DIM, :])

    yz = y_ref[...] * _silu(zx[:, :D_INNER])
    gs = D_INNER // M_GROUPS
    parts = []
    for g in range(M_GROUPS):
        blk = yz[:, g * gs:(g + 1) * gs]
        parts.append(_rmsnorm(blk, gnw_ref[:, g * gs:(g + 1) * gs]))
    yn = jnp.concatenate(parts, axis=1).astype(BF16)
    h_ref[...] = (x + _dot(yn, wout_ref[...])).reshape(nb, L, D_MODEL)


def _mamba_layer(x, conv0, ssm0, w, *, nb, L, Lc):
    B, Ltot, _ = x.shape
    grid = (B // nb, Ltot // L)
    consts = [w["norm"], w["w_in"], w["w_dtT"], w["conv_w"], w["conv_b"], w["dt_b"], w["dt_bT"],
              w["a_log"], w["a_logT"], w["d_x"], w["gn_w"], w["w_out"], w["ex"]]
    return pl.pallas_call(
        functools.partial(_mamba_kernel, nb=nb, L=L, Lc=Lc),
        grid=grid,
        in_specs=[pl.BlockSpec((nb, L, D_MODEL), lambda s, t: (s, t, 0)),
                  pl.BlockSpec((nb, M_CONV_W - 1, CONV_DIM), lambda s, t: (s, 0, 0)),
                  pl.BlockSpec((nb, M_HEADS, M_HEAD_DIM, D_STATE), lambda s, t: (s, 0, 0, 0))]
        + [_const_spec(c.shape) for c in consts],
        out_specs=[pl.BlockSpec((nb, L, D_MODEL), lambda s, t: (s, t, 0)),
                   pl.BlockSpec((nb, M_CONV_W - 1, CONV_DIM), lambda s, t: (s, 0, 0)),
                   pl.BlockSpec((nb, M_HEADS, M_HEAD_DIM, D_STATE), lambda s, t: (s, 0, 0, 0))],
        out_shape=[jax.ShapeDtypeStruct(x.shape, F32),
                   jax.ShapeDtypeStruct(conv0.shape, F32),
                   jax.ShapeDtypeStruct(ssm0.shape, F32)],
        scratch_shapes=[pltpu.VMEM((nb, HALO + L, CONV_DIM), F32),
                        pltpu.VMEM((nb * L, D_INNER), F32)],
        compiler_params=_params(),
        name="mamba_layer",
    )(x, conv0, ssm0, *consts)


def _ffn_kernel(*refs, nb, L, with_attn, final):
    it = iter(refs)
    h_ref = next(it)
    if with_attn:
        o_ref, wo_ref = next(it), next(it)
    fconv0_ref, nw_ref, wup_ref, cw_ref, cb_ref, wdown_ref = (next(it) for _ in range(6))
    if final:
        fnw_ref = next(it)
    out_ref, fconvn_ref, fb_ref = next(it), next(it), next(it)
    M = nb * L
    lo = HALO - (F_CONV_W - 1)

    @pl.when(pl.program_id(1) == 0)
    def _():
        fb_ref[:, lo:HALO, :] = fconv0_ref[...]

    h = h_ref[...].reshape(M, D_MODEL)
    if with_attn:
        h = h + _dot(o_ref[...].reshape(M, D_ATTN), wo_ref[...])
    ub = _rmsnorm(h, nw_ref[...]).astype(BF16)
    fb_ref[:, HALO:HALO + L, :] = _dot(ub, wup_ref[...]).reshape(nb, L, 2 * D_FF)
    acc = cb_ref[...]
    for tap in range(F_CONV_W):
        acc = acc + fb_ref[:, lo + tap:lo + tap + L, :] * cw_ref[tap:tap + 1, :]
    halo = fb_ref[:, lo + L:HALO + L, :]
    fb_ref[:, lo:HALO, :] = halo
    fconvn_ref[...] = halo
    a = acc.reshape(M, 2 * D_FF)
    act = (_silu(a[:, :D_FF]) * a[:, D_FF:]).astype(BF16)
    hn = h + _dot(act, wdown_ref[...])
    if final:
        hn = _rmsnorm(hn, fnw_ref[...])
    out_ref[...] = hn.reshape(nb, L, D_MODEL)


def _ffn_layer(h, fconv0, w, *, nb, L, o=None, w_o=None, final_w=None):
    B, Ltot, _ = h.shape
    grid = (B // nb, Ltot // L)
    with_attn, final = o is not None, final_w is not None
    tile = pl.BlockSpec((nb, L, D_MODEL), lambda s, t: (s, t, 0))
    args, specs = [h], [tile]
    if with_attn:
        args += [o, w_o]
        specs += [pl.BlockSpec((nb, L, D_ATTN), lambda s, t: (s, t, 0)), _const_spec(w_o.shape)]
    consts = [w["norm"], w["w_up"], w["conv_w"], w["conv_b"], w["w_down"]]
    args += [fconv0] + consts
    specs += [pl.BlockSpec((nb, F_CONV_W - 1, 2 * D_FF), lambda s, t: (s, 0, 0))]
    specs += [_const_spec(c.shape) for c in consts]
    if final:
        args.append(final_w)
        specs.append(_const_spec(final_w.shape))
    return pl.pallas_call(
        functools.partial(_ffn_kernel, nb=nb, L=L, with_attn=with_attn, final=final),
        grid=grid,
        in_specs=specs,
        out_specs=[tile, pl.BlockSpec((nb, F_CONV_W - 1, 2 * D_FF), lambda s, t: (s, 0, 0))],
        out_shape=[jax.ShapeDtypeStruct(h.shape, F32), jax.ShapeDtypeStruct(fconv0.shape, F32)],
        scratch_shapes=[pltpu.VMEM((nb, HALO + L, 2 * D_FF), F32)],
        compiler_params=_params(),
        name="conv_ffn",
    )(*args)


def _log_sigmoid(x):
    return -_softplus(-x)


def _proj_prompt_kernel(h_ref, kvw_ref, nmw_ref, wk_ref, wv_ref, wvT_ref, wf_ref, bf_ref,
                        wkp_ref, wqp_ref, pk_ref, pq_ref, onek_ref, oneq_ref,
                        k_ref, v_ref, logf_ref, qa_ref, ka_ref, vT_ref, carry_ref, *, T):
    @pl.when(pl.program_id(0) == 0)
    def _():
        carry_ref[...] = jnp.zeros_like(carry_ref)

    h = h_ref[...]
    hk = _rmsnorm(h, kvw_ref[...]).astype(BF16)
    u1 = _rmsnorm(h, nmw_ref[...]).astype(BF16)
    k_ref[...] = _dot(hk, wk_ref[...])
    v_ref[...] = _dot(hk, wv_ref[...])
    vT_ref[0] = _dot_nt(wvT_ref[...], hk).astype(BF16)
    logf = _log_sigmoid(_dot(hk, wf_ref[...])[:, :N_HEADS] + bf_ref[...])
    logf_ref[...] = logf
    c = carry_ref[...] + _sel_left(_tri(T, True).astype(BF16), logf)
    carry_ref[...] = c[T - 1:T, :]
    c_hi, c_mid, c_lo = _split3(c * LOG2E)
    ka = _dot(hk, wkp_ref[...]) + onek_ref[...]
    qa = _dot(u1, wqp_ref[...]) + oneq_ref[...]
    for i, part in enumerate((c_hi, c_mid, c_lo)):
        ka = ka + _dot(part, pk_ref[i])
        qa = qa + _dot(part, pq_ref[i])
    ka_ref[...] = ka.astype(BF16)
    qa_ref[...] = qa.astype(BF16)


def _proj_prompt(h2d, w, *, T):
    Ltot = h2d.shape[0]
    consts = [w["kv_norm"], w["mix_norm"], w["w_k"], w["w_v"], w["w_vT"], w["w_f"], w["b_f"],
              w["w_k_aug"], w["w_q_aug"], w["p_k"], w["p_q"], w["one_k"], w["one_q"]]
    row = lambda n: pl.BlockSpec((T, n), lambda i: (i, 0))
    return pl.pallas_call(
        functools.partial(_proj_prompt_kernel, T=T),
        grid=(Ltot // T,),
        in_specs=[row(D_MODEL)] + [_const_spec(c.shape) for c in consts],
        out_specs=[row(D_ATTN), row(D_ATTN), row(N_HEADS), row(N_HEADS * AUG), row(N_HEADS * AUG),
                   pl.BlockSpec((1, D_ATTN, T), lambda i: (i, 0, 0))],
        out_shape=[jax.ShapeDtypeStruct((Ltot, D_ATTN), F32),
                   jax.ShapeDtypeStruct((Ltot, D_ATTN), F32),
                   jax.ShapeDtypeStruct((Ltot, N_HEADS), F32),
                   jax.ShapeDtypeStruct((Ltot, N_HEADS * AUG), BF16),
                   jax.ShapeDtypeStruct((Ltot, N_HEADS * AUG), BF16),
                   jax.ShapeDtypeStruct((Ltot // T, D_ATTN, T), BF16)],
        scratch_shapes=[pltpu.VMEM((1, N_HEADS), F32)],
        compiler_params=_params(1),
        name="kvq_proj_prompt",
    )(h2d, *consts)


def _proj_sample_kernel(h_ref, kvw_ref, nmw_ref, wk_ref, wv_ref, wf_ref, bf_ref, wq_ref,
                        k_ref, v_ref, logf_ref, q_ref):
    h = h_ref[...]
    hk = _rmsnorm(h, kvw_ref[...]).astype(BF16)
    u1 = _rmsnorm(h, nmw_ref[...]).astype(BF16)
    k_ref[...] = _dot(hk, wk_ref[...])
    v_ref[...] = _dot(hk, wv_ref[...])
    logf_ref[...] = _log_sigmoid(_dot(hk, wf_ref[...])[:, :N_HEADS] + bf_ref[...])
    q_ref[...] = _dot(u1, wq_ref[...]).astype(BF16)


def _proj_sample(h2d, w, *, T):
    n = h2d.shape[0]
    consts = [w["kv_norm"], w["mix_norm"], w["w_k"], w["w_v"], w["w_f"], w["b_f"], w["w_q"]]
    row = lambda m: pl.BlockSpec((T, m), lambda i: (i, 0))
    return pl.pallas_call(
        _proj_sample_kernel,
        grid=(n // T,),
        in_specs=[row(D_MODEL)] + [_const_spec(c.shape) for c in consts],
        out_specs=[row(D_ATTN), row(D_ATTN), row(N_HEADS), row(D_ATTN)],
        out_shape=[jax.ShapeDtypeStruct((n, D_ATTN), F32), jax.ShapeDtypeStruct((n, D_ATTN), F32),
                   jax.ShapeDtypeStruct((n, N_HEADS), F32), jax.ShapeDtypeStruct((n, D_ATTN), BF16)],
        compiler_params=_params(1),
        name="kvq_proj_sample",
    )(h2d, *consts)


def _attn_prompt_kernel(qa_ref, ka_ref, vT_ref, o_ref, s_ref, acc_ref, *, T, nblk):
    qi = pl.program_id(1)
    heads = range(ATTN_HEADS)
    q = [qa_ref[:, hh * AUG:(hh + 1) * AUG] for hh in heads]
    key_minus_query = (lax.broadcasted_iota(jnp.int32, (T, T), 0)
                       - lax.broadcasted_iota(jnp.int32, (T, T), 1))
    ones = jnp.ones((BF16_SUBLANES, T), BF16)

    def produce(kb, slot):
        start = pl.multiple_of(jnp.minimum(kb, nblk - 1) * T, T)
        for hh in heads:
            s_ref[slot, hh] = _dot_nt(ka_ref[pl.ds(start, T), hh * AUG:(hh + 1) * AUG], q[hh])

    def consume(kb, slot, carry, masked):
        out = []
        for hh in heads:
            m, acc = carry[hh]
            sT = s_ref[slot, hh]
            if masked:
                sT = jnp.where(key_minus_query <= (qi - kb) * T, sT, -jnp.inf)
            m_new = jnp.maximum(m, jnp.max(sT, axis=0, keepdims=True))
            pT = jnp.exp2(sT - m_new).astype(BF16)
            vT = vT_ref[jnp.minimum(kb, nblk - 1), hh * HEAD_DIM:(hh + 1) * HEAD_DIM, :]
            acc = jnp.exp2(m - m_new) * acc + _dot(jnp.concatenate([vT, ones], axis=0), pT)
            out.append((m_new, acc))
        return tuple(out)

    def step(kb, slot, carry, masked):
        produce(kb + 1, 1 - slot)
        return consume(kb, slot, carry, masked)

    def pair(i, carry):
        return step(2 * i + 1, 1, step(2 * i, 0, carry, False), False)

    init = tuple((jnp.full((1, T), -jnp.inf, F32), jnp.zeros((HEAD_DIM + BF16_SUBLANES, T), F32))
                 for _ in heads)
    produce(0, 0)
    carry = lax.fori_loop(0, qi // 2, pair, init)
    last = 2 * (qi // 2)
    carry = step(last, 0, carry, True)
    carry = consume(last + 1, 1, carry, True)
    for hh in heads:
        acc = carry[hh][1]
        acc_ref[hh * HEAD_DIM:(hh + 1) * HEAD_DIM, :] = acc[:HEAD_DIM] * (1.0 / acc[HEAD_DIM:HEAD_DIM + 1])
    o_ref[...] = acc_ref[...].T.astype(BF16)


def _attn_prompt(qa, ka, vT, *, T):
    Ltot = qa.shape[0]
    nblk = Ltot // T
    hps = ATTN_HEADS
    return pl.pallas_call(
        functools.partial(_attn_prompt_kernel, T=T, nblk=nblk),
        grid=(N_HEADS // hps, nblk),
        in_specs=[pl.BlockSpec((T, hps * AUG), lambda p, i: (i, p)),
                  pl.BlockSpec((Ltot, hps * AUG), lambda p, i: (0, p), pipeline_mode=pl.Buffered(1)),
                  pl.BlockSpec((nblk, hps * HEAD_DIM, T), lambda p, i: (0, p, 0),
                               pipeline_mode=pl.Buffered(1))],
        out_specs=pl.BlockSpec((T, hps * HEAD_DIM), lambda p, i: (i, p)),
        out_shape=jax.ShapeDtypeStruct((Ltot, D_ATTN), BF16),
        scratch_shapes=[pltpu.VMEM((2, hps, T, T), F32), pltpu.VMEM((hps * HEAD_DIM, T), F32)],
        compiler_params=_params(),
        name="fox_attn_prompt",
    )(qa, ka, vT)


def _attn_sample_kernel(q_ref, kn_ref, vn_ref, lfn_ref, ck_ref, cv_ref, lfT_ref, o_ref, *, l, P):
    HQ = 4
    R = HQ * l
    lane_p = lax.broadcasted_iota(jnp.int32, (N_HEADS, P), 1)
    suffix = lfT_ref[0]
    suffix = jnp.where(lane_p < P - 1, pltpu.roll(suffix, P - 1, axis=1), 0.0)
    d = 1
    while d < P:
        suffix = suffix + jnp.where(lane_p < P - d, pltpu.roll(suffix, P - d, axis=1), 0.0)
        d *= 2
    lfn = lfn_ref[0]
    cn = _sel_left(_tri(l, True).astype(BF16), lfn)
    cn_rows = jnp.concatenate([cn] * HQ, axis=0)
    row_head = lax.broadcasted_iota(jnp.int32, (R, N_HEADS), 0) // l
    lane_head = lax.broadcasted_iota(jnp.int32, (R, N_HEADS), 1)
    row_q = lax.broadcasted_iota(jnp.int32, (R, HQ * HEAD_DIM), 0) // l
    lane_q = lax.broadcasted_iota(jnp.int32, (R, HQ * HEAD_DIM), 1) // HEAD_DIM
    own = row_q == lane_q
    t_row = lax.broadcasted_iota(jnp.int32, (R, l), 0) % l
    j_col = lax.broadcasted_iota(jnp.int32, (R, l), 1)
    q = q_ref[0]
    outs = []
    for qd in range(N_HEADS // HQ):
        cols = slice(qd * HQ * HEAD_DIM, (qd + 1) * HQ * HEAD_DIM)
        sel = lane_head == row_head + qd * HQ
        qx = jnp.concatenate([q[:, cols]] * HQ, axis=0)
        qx = jnp.where(own, qx, jnp.zeros_like(qx))
        cn_col = jnp.sum(jnp.where(sel, cn_rows, 0.0), axis=1, keepdims=True)
        cn_keys = _sel_right_nt(sel.astype(BF16), cn)
        past_bias = jnp.concatenate(
            [jnp.broadcast_to(suffix[qd * HQ + j:qd * HQ + j + 1, :], (l, P)) for j in range(HQ)], axis=0)
        s_p = _dot_nt(qx, ck_ref[0, :, cols].astype(BF16)) + past_bias + cn_col
        s_n = _dot_nt(qx, kn_ref[0, :, cols].astype(BF16)) + cn_col - cn_keys
        s_n = jnp.where(j_col <= t_row, s_n, -jnp.inf)
        m = jnp.maximum(jnp.max(s_p, axis=1, keepdims=True), jnp.max(s_n, axis=1, keepdims=True))
        p_p = jnp.exp(s_p - m)
        p_n = jnp.exp(s_n - m)
        den = jnp.sum(p_p, axis=1, keepdims=True) + jnp.sum(p_n, axis=1, keepdims=True)
        o = (_dot(p_p.astype(BF16), cv_ref[0, :, cols].astype(BF16))
             + _dot(p_n.astype(BF16), vn_ref[0, :, cols].astype(BF16))) * (1.0 / den)
        o = jnp.where(own, o, 0.0)
        acc = o[0:l]
        for j in range(1, HQ):
            acc = acc + o[j * l:(j + 1) * l]
        outs.append(acc)
    o_ref[0] = jnp.concatenate(outs, axis=1).astype(BF16)


def _attn_sample(q, kn, vn, lfn, cache_k, cache_v, cache_lfT):
    B, l, _ = q.shape
    P = cache_k.shape[1]
    seq = lambda a, b: pl.BlockSpec((1, a, b), lambda i: (i, 0, 0))
    return pl.pallas_call(
        functools.partial(_attn_sample_kernel, l=l, P=P),
        grid=(B,),
        in_specs=[seq(l, D_ATTN), seq(l, D_ATTN), seq(l, D_ATTN), seq(l, N_HEADS),
                  seq(P, D_ATTN), seq(P, D_ATTN), seq(N_HEADS, P)],
        out_specs=seq(l, D_ATTN),
        out_shape=jax.ShapeDtypeStruct((B, l, D_ATTN), BF16),
        compiler_params=_params(1),
        name="fox_attn_sample",
    )(q, kn, vn, lfn, cache_k, cache_v, cache_lfT)


def _row(v):
    return v.reshape(1, -1).astype(F32)


def _col(v):
    return v.reshape(-1, 1).astype(F32)


def _prep_weights(norm_mix_w, norm_ffn_w, mamba_w_in, mamba_conv_w, mamba_conv_b, mamba_dt_bias,
                  mamba_A_log, mamba_D, mamba_norm_w, mamba_w_out, kv_norm_w, w_k, w_v, w_f, b_f,
                  attn_w_q, attn_w_o, ffn_w_up, ffn_conv_w, ffn_conv_b, ffn_w_down, final_norm_w):
    w_in = mamba_w_in[0]
    mamba = {
        "norm": _row(norm_mix_w[0]),
        "w_in": jnp.pad(w_in, ((0, 0), (0, IN_PROJ_PAD - IN_PROJ_DIM))).astype(BF16),
        "w_dtT": w_in[:, D_INNER + CONV_DIM:].T.astype(BF16),
        "conv_w": mamba_conv_w[0], "conv_b": _row(mamba_conv_b[0]),
        "dt_b": _row(mamba_dt_bias[0]), "dt_bT": _col(mamba_dt_bias[0]),
        "a_log": _row(mamba_A_log[0]), "a_logT": _col(mamba_A_log[0]),
        "d_x": _row(jnp.repeat(mamba_D[0], M_HEAD_DIM)),
        "gn_w": _row(mamba_norm_w[0]),
        "w_out": mamba_w_out[0].astype(BF16),
        "ex": jnp.repeat(jnp.eye(M_HEADS, dtype=BF16), M_HEAD_DIM, axis=1),
    }
    ffn = [{"norm": _row(norm_ffn_w[i]), "w_up": ffn_w_up[i].astype(BF16), "conv_w": ffn_conv_w[i],
            "conv_b": _row(ffn_conv_b[i]), "w_down": ffn_w_down[i].astype(BF16)} for i in range(2)]

    def per_head_pad(wm):
        wm = wm.reshape(D_MODEL, N_HEADS, HEAD_DIM)
        return jnp.pad(wm, ((0, 0), (0, 0), (0, AUG - HEAD_DIM))).reshape(D_MODEL, N_HEADS * AUG)

    def place(offset, value):
        m = jnp.zeros((3, N_HEADS, N_HEADS, AUG), F32)
        for i in range(3):
            m = m.at[i, :, :, offset + i].set(value * jnp.eye(N_HEADS, dtype=F32))
        return m.reshape(3, N_HEADS, N_HEADS * AUG).astype(BF16)

    def ones_at(offset):
        v = jnp.zeros((N_HEADS, AUG), F32).at[:, offset:offset + 3].set(1.0)
        return v.reshape(1, N_HEADS * AUG)

    w_q = attn_w_q[0] * ATTN_SCALE
    proj = {
        "kv_norm": _row(kv_norm_w), "mix_norm": _row(norm_mix_w[1]),
        "w_k": w_k.astype(BF16), "w_v": w_v.astype(BF16), "w_vT": w_v.T.astype(BF16),
        "w_f": jnp.pad(w_f, ((0, 0), (0, LANE - N_HEADS))).astype(BF16), "b_f": _row(b_f),
        "w_q": w_q.astype(BF16),
        "w_k_aug": per_head_pad(w_k).astype(BF16), "w_q_aug": per_head_pad(w_q * LOG2E).astype(BF16),
        "p_k": place(HEAD_DIM, -1.0), "p_q": place(HEAD_DIM + 3, 1.0),
        "one_k": ones_at(HEAD_DIM + 3), "one_q": ones_at(HEAD_DIM),
    }
    return mamba, ffn, proj, attn_w_o[0].astype(BF16), _row(final_norm_w)


PROMPT_TILE = 256
PROMPT_CHUNK = 128
ATTN_TILE = 256
ATTN_HEADS = 4
BF16_SUBLANES = 16
SAMPLE_SEQS_MAMBA = 4
SAMPLE_SEQS_FFN = 16


def _trunk_prompt(x, weights):
    mamba, ffn, proj, w_o, final_w = weights
    B, Ltot, _ = x.shape
    T = min(PROMPT_TILE, Ltot)
    conv0 = jnp.zeros((B, M_CONV_W - 1, CONV_DIM), F32)
    ssm0 = jnp.zeros((B, M_HEADS, M_HEAD_DIM, D_STATE), F32)
    fconv0 = jnp.zeros((B, F_CONV_W - 1, 2 * D_FF), F32)
    h, mconv, ssm = _mamba_layer(x, conv0, ssm0, mamba, nb=1, L=T, Lc=min(PROMPT_CHUNK, T))
    h, fconv_a = _ffn_layer(h, fconv0, ffn[0], nb=1, L=T)
    TA = min(ATTN_TILE, Ltot)
    k, v, logf, qa, ka, vT = _proj_prompt(h.reshape(B * Ltot, D_MODEL), proj, T=TA)
    o = _attn_prompt(qa, ka, vT, T=TA)
    y, fconv_b = _ffn_layer(h, fconv0, ffn[1], nb=1, L=T, o=o.reshape(B, Ltot, D_ATTN), w_o=w_o,
                            final_w=final_w)
    return (y, ssm[None], mconv[None], jnp.stack([fconv_a, fconv_b]),
            k.reshape(B, Ltot, N_HEADS, HEAD_DIM), v.reshape(B, Ltot, N_HEADS, HEAD_DIM),
            logf.reshape(B, Ltot, N_HEADS))


def _trunk_sample(x, state_ssm, state_mconv, state_fconv, cache_k, cache_v, cache_logf, weights):
    mamba, ffn, proj, w_o, final_w = weights
    B, l, _ = x.shape
    P = cache_k.shape[1]
    h, mconv, ssm = _mamba_layer(x, state_mconv[0], state_ssm[0], mamba,
                                 nb=min(SAMPLE_SEQS_MAMBA, B), L=l, Lc=l)
    nbf = min(SAMPLE_SEQS_FFN, B)
    h, fconv_a = _ffn_layer(h, state_fconv[0], ffn[0], nb=nbf, L=l)
    k, v, logf, q = _proj_sample(h.reshape(B * l, D_MODEL), proj, T=nbf * l)
    o = _attn_sample(q.reshape(B, l, D_ATTN), k.reshape(B, l, D_ATTN), v.reshape(B, l, D_ATTN),
                     logf.reshape(B, l, N_HEADS), cache_k.reshape(B, P, D_ATTN),
                     cache_v.reshape(B, P, D_ATTN), jnp.swapaxes(cache_logf, 1, 2))
    y, fconv_b = _ffn_layer(h, state_fconv[1], ffn[1], nb=nbf, L=l, o=o, w_o=w_o, final_w=final_w)
    return (y, ssm[None], mconv[None], jnp.stack([fconv_a, fconv_b]),
            k.reshape(B, l, N_HEADS, HEAD_DIM), v.reshape(B, l, N_HEADS, HEAD_DIM),
            logf.reshape(B, l, N_HEADS))


def kernel(x_prompt, x_sample, state_ssm, state_mamba_conv, state_ffn_conv, cache_k, cache_v, cache_logf,
           norm_mix_w, norm_ffn_w, mamba_w_in, mamba_conv_w, mamba_conv_b, mamba_dt_bias, mamba_A_log,
           mamba_D, mamba_norm_w, mamba_w_out, kv_norm_w, w_k, w_v, w_f, b_f, attn_w_q, attn_w_o,
           ffn_w_up, ffn_conv_w, ffn_conv_b, ffn_w_down, final_norm_w):
    weights = _prep_weights(norm_mix_w, norm_ffn_w, mamba_w_in, mamba_conv_w, mamba_conv_b,
                            mamba_dt_bias, mamba_A_log, mamba_D, mamba_norm_w, mamba_w_out, kv_norm_w,
                            w_k, w_v, w_f, b_f, attn_w_q, attn_w_o, ffn_w_up, ffn_conv_w, ffn_conv_b,
                            ffn_w_down, final_norm_w)
    p = _trunk_prompt(x_prompt, weights)
    s = _trunk_sample(x_sample, state_ssm, state_mamba_conv, state_ffn_conv, cache_k, cache_v,
                      cache_logf, weights)
    return (p[0], s[0]) + p[1:] + s[1:]
```

```python
import functools

import jax
import jax.numpy as jnp
from jax import lax
from jax.experimental import pallas as pl
from jax.experimental.pallas import tpu as pltpu

F32 = jnp.float32
BF16 = jnp.bfloat16

EPS = 1e-6
D_MODEL = 1024
D_INNER = 2048
M_HEAD_DIM = 64
M_HEADS = 32
M_GROUPS = 4
HEADS_PER_GROUP = M_HEADS // M_GROUPS
D_STATE = 128
M_CONV_W = 4
CONV_DIM = D_INNER + 2 * M_GROUPS * D_STATE
IN_PROJ_DIM = D_INNER + CONV_DIM + M_HEADS
LANE = 128
IN_PROJ_PAD = -(-IN_PROJ_DIM // LANE) * LANE
HEAD_DIM = 64
N_HEADS = 16
D_ATTN = N_HEADS * HEAD_DIM
ATTN_SCALE = HEAD_DIM ** -0.5
LOG2E = 1.4426950408889634
D_FF = 2816
F_CONV_W = 3
HALO = 8
AUG = 2 * HEAD_DIM
VMEM_LIMIT = 56 * 1024 * 1024


def _dot(a, b):
    return jnp.dot(a, b, preferred_element_type=F32)


def _dot_nt(a, b):
    return lax.dot_general(a, b, (((1,), (1,)), ((), ())), preferred_element_type=F32)


def _dot_tn(a, b):
    return lax.dot_general(a, b, (((0,), (0,)), ((), ())), preferred_element_type=F32)


def _split3(x):
    hi = x.astype(BF16)
    r = x - hi.astype(F32)
    mid = r.astype(BF16)
    lo = (r - mid.astype(F32)).astype(BF16)
    return hi, mid, lo


def _sel_left(sel, x):
    hi, mid, lo = _split3(x)
    return _dot(sel, hi) + _dot(sel, mid) + _dot(sel, lo)


def _sel_right(x, sel):
    hi, mid, lo = _split3(x)
    return _dot(hi, sel) + _dot(mid, sel) + _dot(lo, sel)


def _sel_right_nt(sel, x):
    hi, mid, lo = _split3(x)
    return _dot_nt(sel, hi) + _dot_nt(sel, mid) + _dot_nt(sel, lo)


def _rmsnorm(x, w):
    return x * lax.rsqrt(jnp.mean(x * x, axis=-1, keepdims=True) + EPS) * w


def _silu(x):
    return x * (1.0 / (1.0 + jnp.exp(-x)))


def _softplus(x):
    return jnp.maximum(x, 0.0) + jnp.log1p(jnp.exp(-jnp.abs(x)))


def _tri(n, lower):
    r = lax.broadcasted_iota(jnp.int32, (n, n), 0)
    c = lax.broadcasted_iota(jnp.int32, (n, n), 1)
    return (r >= c) if lower else (r <= c)


def _const_spec(shape):
    nd = len(shape)
    return pl.BlockSpec(shape, lambda *_: (0,) * nd, pipeline_mode=pl.Buffered(1))


def _params(n_axes=2):
    return pltpu.CompilerParams(dimension_semantics=("arbitrary",) * n_axes,
                                vmem_limit_bytes=VMEM_LIMIT)


def _mamba_kernel(x_ref, conv0_ref, ssm0_ref, nw_ref, win_ref, wdtT_ref, cw_ref, cb_ref,
                  dtb_ref, dtbT_ref, alog_ref, alogT_ref, dx_ref, gnw_ref, wout_ref, ex_ref,
                  h_ref, convn_ref, ssmn_ref, fb_ref, y_ref, *, nb, L, Lc):
    M = nb * L
    lo = HALO - (M_CONV_W - 1)

    @pl.when(pl.program_id(1) == 0)
    def _():
        fb_ref[:, lo:HALO, :] = conv0_ref[...]
        ssmn_ref[...] = ssm0_ref[...]

    x = x_ref[...].reshape(M, D_MODEL)
    ub = _rmsnorm(x, nw_ref[...]).astype(BF16)
    zx = _dot(ub, win_ref[...])
    dtT = _softplus(_dot_nt(wdtT_ref[...], ub) + dtbT_ref[...])
    dt = _softplus(zx[:, D_INNER + CONV_DIM:IN_PROJ_DIM] + dtb_ref[...])

    fb_ref[:, HALO:HALO + L, :] = zx[:, D_INNER:D_INNER + CONV_DIM].reshape(nb, L, CONV_DIM)
    acc = cb_ref[...]
    for tap in range(M_CONV_W):
        acc = acc + fb_ref[:, lo + tap:lo + tap + L, :] * cw_ref[tap:tap + 1, :]
    halo = fb_ref[:, lo + L:HALO + L, :]
    fb_ref[:, lo:HALO, :] = halo
    convn_ref[...] = halo
    xbc = _silu(acc).reshape(M, CONV_DIM)
    xs = xbc[:, :D_INNER]
    bm = xbc[:, D_INNER:D_INNER + M_GROUPS * D_STATE].astype(BF16)
    cm = xbc[:, D_INNER + M_GROUPS * D_STATE:].astype(BF16)

    dA = dt * (-jnp.exp(alog_ref[...]))
    dAT = dtT * (-jnp.exp(alogT_ref[...]))
    tril = _tri(Lc, True)
    tril_b = tril.astype(BF16)
    triu_b = _tri(Lc, False).astype(BF16)
    ex = ex_ref[...]
    quad = lax.broadcasted_iota(jnp.int32, (Lc, 4 * M_HEAD_DIM), 1) // M_HEAD_DIM

    for s in range(nb):
        for c in range(L // Lc):
            r0 = s * L + c * Lc
            acum = _sel_left(tril_b, dA[r0:r0 + Lc])
            dtT_c = dtT[:, r0:r0 + Lc]
            acumT = _sel_right(dAT[:, r0:r0 + Lc], triu_b)
            acum_x = _sel_right(acum, ex)
            dt_x = _sel_right(dt[r0:r0 + Lc], ex)
            xs_c = xs[r0:r0 + Lc]
            xs_b = xs_c.astype(BF16)
            x_end = (xs_c * (jnp.exp(acum_x[Lc - 1:Lc, :] - acum_x) * dt_x)).astype(BF16)
            e_x = jnp.exp(acum_x)
            d_blk = jnp.exp(acumT[:, Lc - 1:Lc])
            for g in range(M_GROUPS):
                b_g = bm[r0:r0 + Lc, g * D_STATE:(g + 1) * D_STATE]
                c_g = cm[r0:r0 + Lc, g * D_STATE:(g + 1) * D_STATE]
                cb = _dot_nt(c_g, b_g)
                h0 = g * HEADS_PER_GROUP
                h_in = ssmn_ref[s, h0:h0 + HEADS_PER_GROUP].reshape(HEADS_PER_GROUP * M_HEAD_DIM, D_STATE)
                y_inter = _dot_nt(c_g, h_in.astype(BF16))
                for qd in range(HEADS_PER_GROUP // 4):
                    col = (h0 + 4 * qd) * M_HEAD_DIM
                    xq = xs_b[:, col:col + 4 * M_HEAD_DIM]
                    yq = jnp.zeros((Lc, 4 * M_HEAD_DIM), F32)
                    for j in range(4):
                        h = h0 + 4 * qd + j
                        seg = acum[:, h:h + 1] - acumT[h:h + 1, :]
                        w = cb * jnp.exp(jnp.where(tril, seg, -jnp.inf)) * dtT_c[h:h + 1, :]
                        yq = yq + _dot(w.astype(BF16), jnp.where(quad == j, xq, jnp.zeros_like(xq)))
                    y_ref[r0:r0 + Lc, col:col + 4 * M_HEAD_DIM] = (
                        yq + y_inter[:, qd * 4 * M_HEAD_DIM:(qd + 1) * 4 * M_HEAD_DIM]
                        * e_x[:, col:col + 4 * M_HEAD_DIM]
                        + dx_ref[:, col:col + 4 * M_HEAD_DIM] * xs_c[:, col:col + 4 * M_HEAD_DIM])
                st = _dot_tn(x_end[:, h0 * M_HEAD_DIM:(h0 + HEADS_PER_GROUP) * M_HEAD_DIM], b_g)
                for j in range(HEADS_PER_GROUP):
                    h = h0 + j
                    ssmn_ref[s, h] = (ssmn_ref[s, h] * d_blk[h:h + 1, :]
                                      + st[j * M_HEAD_DIM:(j + 1) * M_HEAD_DIM, :])

    yz = y_ref[...] * _silu(zx[:, :D_INNER])
    gs = D_INNER // M_GROUPS
    parts = []
    for g in range(M_GROUPS):
        blk = yz[:, g * gs:(g + 1) * gs]
        parts.append(_rmsnorm(blk, gnw_ref[:, g * gs:(g + 1) * gs]))
    yn = jnp.concatenate(parts, axis=1).astype(BF16)
    h_ref[...] = (x + _dot(yn, wout_ref[...])).reshape(nb, L, D_MODEL)


def _mamba_layer(x, conv0, ssm0, w, *, nb, L, Lc):
    B, Ltot, _ = x.shape
    grid = (B // nb, Ltot // L)
    consts = [w["norm"], w["w_in"], w["w_dtT"], w["conv_w"], w["conv_b"], w["dt_b"], w["dt_bT"],
              w["a_log"], w["a_logT"], w["d_x"], w["gn_w"], w["w_out"], w["ex"]]
    return pl.pallas_call(
        functools.partial(_mamba_kernel, nb=nb, L=L, Lc=Lc),
        grid=grid,
        in_specs=[pl.BlockSpec((nb, L, D_MODEL), lambda s, t: (s, t, 0)),
                  pl.BlockSpec((nb, M_CONV_W - 1, CONV_DIM), lambda s, t: (s, 0, 0)),
                  pl.BlockSpec((nb, M_HEADS, M_HEAD_DIM, D_STATE), lambda s, t: (s, 0, 0, 0))]
        + [_const_spec(c.shape) for c in consts],
        out_specs=[pl.BlockSpec((nb, L, D_MODEL), lambda s, t: (s, t, 0)),
                   pl.BlockSpec((nb, M_CONV_W - 1, CONV_DIM), lambda s, t: (s, 0, 0)),
                   pl.BlockSpec((nb, M_HEADS, M_HEAD_DIM, D_STATE), lambda s, t: (s, 0, 0, 0))],
        out_shape=[jax.ShapeDtypeStruct(x.shape, F32),
                   jax.ShapeDtypeStruct(conv0.shape, F32),
                   jax.ShapeDtypeStruct(ssm0.shape, F32)],
        scratch_shapes=[pltpu.VMEM((nb, HALO + L, CONV_DIM), F32),
                        pltpu.VMEM((nb * L, D_INNER), F32)],
        compiler_params=_params(),
        name="mamba_layer",
    )(x, conv0, ssm0, *consts)


def _ffn_kernel(*refs, nb, L, with_attn, final):
    it = iter(refs)
    h_ref = next(it)
    if with_attn:
        o_ref, wo_ref = next(it), next(it)
    fconv0_ref, nw_ref, wup_ref, cw_ref, cb_ref, wdown_ref = (next(it) for _ in range(6))
    if final:
        fnw_ref = next(it)
    out_ref, fconvn_ref, fb_ref = next(it), next(it), next(it)
    M = nb * L
    lo = HALO - (F_CONV_W - 1)

    @pl.when(pl.program_id(1) == 0)
    def _():
        fb_ref[:, lo:HALO, :] = fconv0_ref[...]

    h = h_ref[...].reshape(M, D_MODEL)
    if with_attn:
        h = h + _dot(o_ref[...].reshape(M, D_ATTN), wo_ref[...])
    ub = _rmsnorm(h, nw_ref[...]).astype(BF16)
    fb_ref[:, HALO:HALO + L, :] = _dot(ub, wup_ref[...]).reshape(nb, L, 2 * D_FF)
    acc = cb_ref[...]
    for tap in range(F_CONV_W):
        acc = acc + fb_ref[:, lo + tap:lo + tap + L, :] * cw_ref[tap:tap + 1, :]
    halo = fb_ref[:, lo + L:HALO + L, :]
    fb_ref[:, lo:HALO, :] = halo
    fconvn_ref[...] = halo
    a = acc.reshape(M, 2 * D_FF)
    act = (_silu(a[:, :D_FF]) * a[:, D_FF:]).astype(BF16)
    hn = h + _dot(act, wdown_ref[...])
    if final:
        hn = _rmsnorm(hn, fnw_ref[...])
    out_ref[...] = hn.reshape(nb, L, D_MODEL)


def _ffn_layer(h, fconv0, w, *, nb, L, o=None, w_o=None, final_w=None):
    B, Ltot, _ = h.shape
    grid = (B // nb, Ltot // L)
    with_attn, final = o is not None, final_w is not None
    tile = pl.BlockSpec((nb, L, D_MODEL), lambda s, t: (s, t, 0))
    args, specs = [h], [tile]
    if with_attn:
        args += [o, w_o]
        specs += [pl.BlockSpec((nb, L, D_ATTN), lambda s, t: (s, t, 0)), _const_spec(w_o.shape)]
    consts = [w["norm"], w["w_up"], w["conv_w"], w["conv_b"], w["w_down"]]
    args += [fconv0] + consts
    specs += [pl.BlockSpec((nb, F_CONV_W - 1, 2 * D_FF), lambda s, t: (s, 0, 0))]
    specs += [_const_spec(c.shape) for c in consts]
    if final:
        args.append(final_w)
        specs.append(_const_spec(final_w.shape))
    return pl.pallas_call(
        functools.partial(_ffn_kernel, nb=nb, L=L, with_attn=with_attn, final=final),
        grid=grid,
        in_specs=specs,
        out_specs=[tile, pl.BlockSpec((nb, F_CONV_W - 1, 2 * D_FF), lambda s, t: (s, 0, 0))],
        out_shape=[jax.ShapeDtypeStruct(h.shape, F32), jax.ShapeDtypeStruct(fconv0.shape, F32)],
        scratch_shapes=[pltpu.VMEM((nb, HALO + L, 2 * D_FF), F32)],
        compiler_params=_params(),
        name="conv_ffn",
    )(*args)


def _log_sigmoid(x):
    return -_softplus(-x)


def _proj_prompt_kernel(h_ref, kvw_ref, nmw_ref, wv_ref, wf_ref, bf_ref,
                        wkp_ref, wqp_ref, pk_ref, pq_ref, onek_ref, oneq_ref,
                        k_ref, v_ref, logf_ref, qa_ref, ka_ref, vT_ref, carry_ref, *, T):
    @pl.when(pl.program_id(0) == 0)
    def _():
        carry_ref[...] = jnp.zeros_like(carry_ref)

    h = h_ref[...]
    hk = _rmsnorm(h, kvw_ref[...]).astype(BF16)
    u1 = _rmsnorm(h, nmw_ref[...]).astype(BF16)
    v = _dot(hk, wv_ref[...])
    v_ref[...] = v
    vT_ref[0] = v.T.astype(BF16)
    kp = _dot(hk, wkp_ref[...])
    k_ref[...] = jnp.concatenate(
        [kp[:, j * AUG:(j + 1) * AUG] + pltpu.roll(kp[:, (j + 1) * AUG:(j + 2) * AUG], HEAD_DIM, axis=1)
         for j in range(0, N_HEADS, 2)], axis=1)
    logf3 = _log_sigmoid(_dot(hk, wf_ref[...]) + bf_ref[...])
    logf_ref[...] = logf3[:, :N_HEADS]
    c3 = carry_ref[...] + _sel_left(_tri(T, True).astype(BF16), logf3)
    carry_ref[...] = c3[T - 1:T, :]
    c_hi, c_mid, c_lo = _split3(c3 * LOG2E)
    group = lax.broadcasted_iota(jnp.int32, (T, LANE), 1) // N_HEADS
    parts = jnp.where(group == 0, c_hi, jnp.where(group == 1, c_mid,
                      jnp.where(group == 2, c_lo, jnp.zeros_like(c_lo))))
    ka_ref[...] = (kp + onek_ref[...] + _dot(parts, pk_ref[...])).astype(BF16)
    qa_ref[...] = (_dot(u1, wqp_ref[...]) + oneq_ref[...] + _dot(parts, pq_ref[...])).astype(BF16)


def _proj_prompt(h2d, w, *, T):
    Ltot = h2d.shape[0]
    consts = [w["kv_norm"], w["mix_norm"], w["w_v"], w["w_f3"], w["b_f3"],
              w["w_k_aug"], w["w_q_aug"], w["p_k"], w["p_q"], w["one_k"], w["one_q"]]
    row = lambda n: pl.BlockSpec((T, n), lambda i: (i, 0))
    return pl.pallas_call(
        functools.partial(_proj_prompt_kernel, T=T),
        grid=(Ltot // T,),
        in_specs=[row(D_MODEL)] + [_const_spec(c.shape) for c in consts],
        out_specs=[row(D_ATTN), row(D_ATTN), row(N_HEADS), row(N_HEADS * AUG), row(N_HEADS * AUG),
                   pl.BlockSpec((1, D_ATTN, T), lambda i: (i, 0, 0))],
        out_shape=[jax.ShapeDtypeStruct((Ltot, D_ATTN), F32),
                   jax.ShapeDtypeStruct((Ltot, D_ATTN), F32),
                   jax.ShapeDtypeStruct((Ltot, N_HEADS), F32),
                   jax.ShapeDtypeStruct((Ltot, N_HEADS * AUG), BF16),
                   jax.ShapeDtypeStruct((Ltot, N_HEADS * AUG), BF16),
                   jax.ShapeDtypeStruct((Ltot // T, D_ATTN, T), BF16)],
        scratch_shapes=[pltpu.VMEM((1, LANE), F32)],
        compiler_params=_params(1),
        name="kvq_proj_prompt",
    )(h2d, *consts)


def _proj_sample_kernel(h_ref, kvw_ref, nmw_ref, wk_ref, wv_ref, wf_ref, bf_ref, wq_ref,
                        k_ref, v_ref, logf_ref, q_ref):
    h = h_ref[...]
    hk = _rmsnorm(h, kvw_ref[...]).astype(BF16)
    u1 = _rmsnorm(h, nmw_ref[...]).astype(BF16)
    k_ref[...] = _dot(hk, wk_ref[...])
    v_ref[...] = _dot(hk, wv_ref[...])
    logf_ref[...] = _log_sigmoid(_dot(hk, wf_ref[...])[:, :N_HEADS] + bf_ref[...])
    q_ref[...] = _dot(u1, wq_ref[...]).astype(BF16)


def _proj_sample(h2d, w, *, T):
    n = h2d.shape[0]
    consts = [w["kv_norm"], w["mix_norm"], w["w_k"], w["w_v"], w["w_f"], w["b_f"], w["w_q"]]
    row = lambda m: pl.BlockSpec((T, m), lambda i: (i, 0))
    return pl.pallas_call(
        _proj_sample_kernel,
        grid=(n // T,),
        in_specs=[row(D_MODEL)] + [_const_spec(c.shape) for c in consts],
        out_specs=[row(D_ATTN), row(D_ATTN), row(N_HEADS), row(D_ATTN)],
        out_shape=[jax.ShapeDtypeStruct((n, D_ATTN), F32), jax.ShapeDtypeStruct((n, D_ATTN), F32),
                   jax.ShapeDtypeStruct((n, N_HEADS), F32), jax.ShapeDtypeStruct((n, D_ATTN), BF16)],
        compiler_params=_params(1),
        name="kvq_proj_sample",
    )(h2d, *consts)


def _attn_prompt_kernel(qa_ref, ka_ref, vT_ref, o_ref, s_ref, acc_ref, oT_ref, *, T, nblk):
    qi = pl.program_id(1)
    heads = range(ATTN_HEADS)
    q = [qa_ref[:, hh * AUG:(hh + 1) * AUG] for hh in heads]
    key_minus_query = (lax.broadcasted_iota(jnp.int32, (T, T), 0)
                       - lax.broadcasted_iota(jnp.int32, (T, T), 1))
    ones = jnp.ones((BF16_SUBLANES, T), BF16)

    def produce(kb, slot):
        start = pl.multiple_of(jnp.minimum(kb, nblk - 1) * T, T)
        bmax = []
        for hh in heads:
            sT = _dot_nt(ka_ref[pl.ds(start, T), hh * AUG:(hh + 1) * AUG], q[hh])
            s_ref[slot, hh] = sT
            bmax.append(jnp.max(sT, axis=0, keepdims=True))
        return tuple(bmax)

    def consume(kb, slot, ms, bmax, masked):
        out = []
        for hh in heads:
            m = ms[hh]
            sT = s_ref[slot, hh]
            if masked:
                sT = jnp.where(key_minus_query <= (qi - kb) * T, sT, -jnp.inf)
                m_new = jnp.maximum(m, jnp.max(sT, axis=0, keepdims=True))
            else:
                m_new = jnp.maximum(m, bmax[hh])
            pT = jnp.exp2(sT - m_new).astype(BF16)
            vT = vT_ref[jnp.minimum(kb, nblk - 1), hh * HEAD_DIM:(hh + 1) * HEAD_DIM, :]
            acc_ref[hh] = jnp.exp2(m - m_new) * acc_ref[hh] + _dot(jnp.concatenate([vT, ones], axis=0), pT)
            out.append(m_new)
        return tuple(out)

    def step(kb, slot, state, masked):
        ms, bmax = state
        bmax_next = produce(kb + 1, 1 - slot)
        return consume(kb, slot, ms, bmax, masked), bmax_next

    def pair(i, state):
        return step(2 * i + 1, 1, step(2 * i, 0, state, False), False)

    acc_ref[...] = jnp.zeros_like(acc_ref)
    init = tuple(jnp.full((1, T), -jnp.inf, F32) for _ in heads)
    n_pairs = qi // 2
    state = lax.fori_loop(0, n_pairs // 2, lambda i, st: pair(2 * i + 1, pair(2 * i, st)),
                          (init, produce(0, 0)))
    state = lax.fori_loop(n_pairs - n_pairs % 2, n_pairs, pair, state)
    last = 2 * n_pairs
    ms, bmax = step(last, 0, state, True)
    consume(last + 1, 1, ms, bmax, True)
    for hh in heads:
        acc = acc_ref[hh]
        oT_ref[hh * HEAD_DIM:(hh + 1) * HEAD_DIM, :] = acc[:HEAD_DIM] * (1.0 / acc[HEAD_DIM:HEAD_DIM + 1])
    o_ref[...] = oT_ref[...].T.astype(BF16)


def _attn_prompt(qa, ka, vT, *, T):
    Ltot = qa.shape[0]
    nblk = Ltot // T
    hps = ATTN_HEADS
    return pl.pallas_call(
        functools.partial(_attn_prompt_kernel, T=T, nblk=nblk),
        grid=(N_HEADS // hps, nblk),
        in_specs=[pl.BlockSpec((T, hps * AUG), lambda p, i: (i, p)),
                  pl.BlockSpec((Ltot, hps * AUG), lambda p, i: (0, p), pipeline_mode=pl.Buffered(1)),
                  pl.BlockSpec((nblk, hps * HEAD_DIM, T), lambda p, i: (0, p, 0),
                               pipeline_mode=pl.Buffered(1))],
        out_specs=pl.BlockSpec((T, hps * HEAD_DIM), lambda p, i: (i, p)),
        out_shape=jax.ShapeDtypeStruct((Ltot, D_ATTN), BF16),
        scratch_shapes=[pltpu.VMEM((2, hps, T, T), F32),
                        pltpu.VMEM((hps, HEAD_DIM + BF16_SUBLANES, T), F32),
                        pltpu.VMEM((hps * HEAD_DIM, T), F32)],
        compiler_params=_params(),
        name="fox_attn_prompt",
    )(qa, ka, vT)


def _attn_sample_kernel(q_ref, kn_ref, vn_ref, lfn_ref, ck_ref, cv_ref, lfT_ref, o_ref, *, l, P):
    HQ = 4
    R = HQ * l
    lane_p = lax.broadcasted_iota(jnp.int32, (N_HEADS, P), 1)
    suffix = lfT_ref[0]
    suffix = jnp.where(lane_p < P - 1, pltpu.roll(suffix, P - 1, axis=1), 0.0)
    d = 1
    while d < P:
        suffix = suffix + jnp.where(lane_p < P - d, pltpu.roll(suffix, P - d, axis=1), 0.0)
        d *= 2
    lfn = lfn_ref[0]
    cn = _sel_left(_tri(l, True).astype(BF16), lfn)
    cn_rows = jnp.concatenate([cn] * HQ, axis=0)
    row_head = lax.broadcasted_iota(jnp.int32, (R, N_HEADS), 0) // l
    lane_head = lax.broadcasted_iota(jnp.int32, (R, N_HEADS), 1)
    row_q = lax.broadcasted_iota(jnp.int32, (R, HQ * HEAD_DIM), 0) // l
    lane_q = lax.broadcasted_iota(jnp.int32, (R, HQ * HEAD_DIM), 1) // HEAD_DIM
    own = row_q == lane_q
    t_row = lax.broadcasted_iota(jnp.int32, (R, l), 0) % l
    j_col = lax.broadcasted_iota(jnp.int32, (R, l), 1)
    q = q_ref[0]
    outs = []
    for qd in range(N_HEADS // HQ):
        cols = slice(qd * HQ * HEAD_DIM, (qd + 1) * HQ * HEAD_DIM)
        sel = lane_head == row_head + qd * HQ
        qx = jnp.concatenate([q[:, cols]] * HQ, axis=0)
        qx = jnp.where(own, qx, jnp.zeros_like(qx))
        cn_col = jnp.sum(jnp.where(sel, cn_rows, 0.0), axis=1, keepdims=True)
        cn_keys = _sel_right_nt(sel.astype(BF16), cn)
        past_bias = jnp.concatenate(
            [jnp.broadcast_to(suffix[qd * HQ + j:qd * HQ + j + 1, :], (l, P)) for j in range(HQ)], axis=0)
        s_p = _dot_nt(qx, ck_ref[0, :, cols].astype(BF16)) + past_bias + cn_col
        s_n = _dot_nt(qx, kn_ref[0, :, cols].astype(BF16)) + cn_col - cn_keys
        s_n = jnp.where(j_col <= t_row, s_n, -jnp.inf)
        m = jnp.maximum(jnp.max(s_p, axis=1, keepdims=True), jnp.max(s_n, axis=1, keepdims=True))
        p_p = jnp.exp(s_p - m)
        p_n = jnp.exp(s_n - m)
        den = jnp.sum(p_p, axis=1, keepdims=True) + jnp.sum(p_n, axis=1, keepdims=True)
        o = (_dot(p_p.astype(BF16), cv_ref[0, :, cols].astype(BF16))
             + _dot(p_n.astype(BF16), vn_ref[0, :, cols].astype(BF16))) * (1.0 / den)
        o = jnp.where(own, o, 0.0)
        acc = o[0:l]
        for j in range(1, HQ):
            acc = acc + o[j * l:(j + 1) * l]
        outs.append(acc)
    o_ref[0] = jnp.concatenate(outs, axis=1).astype(BF16)


def _attn_sample(q, kn, vn, lfn, cache_k, cache_v, cache_lfT):
    B, l, _ = q.shape
    P = cache_k.shape[1]
    seq = lambda a, b: pl.BlockSpec((1, a, b), lambda i: (i, 0, 0))
    return pl.pallas_call(
        functools.partial(_attn_sample_kernel, l=l, P=P),
        grid=(B,),
        in_specs=[seq(l, D_ATTN), seq(l, D_ATTN), seq(l, D_ATTN), seq(l, N_HEADS),
                  seq(P, D_ATTN), seq(P, D_ATTN), seq(N_HEADS, P)],
        out_specs=seq(l, D_ATTN),
        out_shape=jax.ShapeDtypeStruct((B, l, D_ATTN), BF16),
        compiler_params=_params(1),
        name="fox_attn_sample",
    )(q, kn, vn, lfn, cache_k, cache_v, cache_lfT)


def _row(v):
    return v.reshape(1, -1).astype(F32)


def _col(v):
    return v.reshape(-1, 1).astype(F32)


def _prep_weights(norm_mix_w, norm_ffn_w, mamba_w_in, mamba_conv_w, mamba_conv_b, mamba_dt_bias,
                  mamba_A_log, mamba_D, mamba_norm_w, mamba_w_out, kv_norm_w, w_k, w_v, w_f, b_f,
                  attn_w_q, attn_w_o, ffn_w_up, ffn_conv_w, ffn_conv_b, ffn_w_down, final_norm_w):
    w_in = mamba_w_in[0]
    mamba = {
        "norm": _row(norm_mix_w[0]),
        "w_in": jnp.pad(w_in, ((0, 0), (0, IN_PROJ_PAD - IN_PROJ_DIM))).astype(BF16),
        "w_dtT": w_in[:, D_INNER + CONV_DIM:].T.astype(BF16),
        "conv_w": mamba_conv_w[0], "conv_b": _row(mamba_conv_b[0]),
        "dt_b": _row(mamba_dt_bias[0]), "dt_bT": _col(mamba_dt_bias[0]),
        "a_log": _row(mamba_A_log[0]), "a_logT": _col(mamba_A_log[0]),
        "d_x": _row(jnp.repeat(mamba_D[0], M_HEAD_DIM)),
        "gn_w": _row(mamba_norm_w[0]),
        "w_out": mamba_w_out[0].astype(BF16),
        "ex": jnp.repeat(jnp.eye(M_HEADS, dtype=BF16), M_HEAD_DIM, axis=1),
    }
    ffn = [{"norm": _row(norm_ffn_w[i]), "w_up": ffn_w_up[i].astype(BF16), "conv_w": ffn_conv_w[i],
            "conv_b": _row(ffn_conv_b[i]), "w_down": ffn_w_down[i].astype(BF16)} for i in range(2)]

    def per_head_pad(wm):
        wm = wm.reshape(D_MODEL, N_HEADS, HEAD_DIM)
        return jnp.pad(wm, ((0, 0), (0, 0), (0, AUG - HEAD_DIM))).reshape(D_MODEL, N_HEADS * AUG)

    def place(offset, value):
        m = jnp.zeros((LANE // N_HEADS, N_HEADS, N_HEADS, AUG), F32)
        for i in range(3):
            m = m.at[i, :, :, offset + i].set(value * jnp.eye(N_HEADS, dtype=F32))
        return m.reshape(LANE, N_HEADS * AUG).astype(BF16)

    def three_groups(a):
        return jnp.pad(jnp.concatenate([a] * 3, axis=-1), ((0, 0), (0, LANE - 3 * N_HEADS)))

    def ones_at(offset):
        v = jnp.zeros((N_HEADS, AUG), F32).at[:, offset:offset + 3].set(1.0)
        return v.reshape(1, N_HEADS * AUG)

    w_q = attn_w_q[0] * ATTN_SCALE
    proj = {
        "kv_norm": _row(kv_norm_w), "mix_norm": _row(norm_mix_w[1]),
        "w_k": w_k.astype(BF16), "w_v": w_v.astype(BF16),
        "w_f": jnp.pad(w_f, ((0, 0), (0, LANE - N_HEADS))).astype(BF16), "b_f": _row(b_f),
        "w_f3": three_groups(w_f).astype(BF16), "b_f3": three_groups(_row(b_f)),
        "w_q": w_q.astype(BF16),
        "w_k_aug": per_head_pad(w_k).astype(BF16), "w_q_aug": per_head_pad(w_q * LOG2E).astype(BF16),
        "p_k": place(HEAD_DIM, -1.0), "p_q": place(HEAD_DIM + 3, 1.0),
        "one_k": ones_at(HEAD_DIM + 3), "one_q": ones_at(HEAD_DIM),
    }
    return mamba, ffn, proj, attn_w_o[0].astype(BF16), _row(final_norm_w)


PROMPT_TILE = 256
PROMPT_CHUNK = 128
ATTN_TILE = 256
ATTN_HEADS = 4
BF16_SUBLANES = 16
SAMPLE_SEQS_MAMBA = 4
SAMPLE_SEQS_FFN = 16


def _trunk_prompt(x, weights):
    mamba, ffn, proj, w_o, final_w = weights
    B, Ltot, _ = x.shape
    T = min(PROMPT_TILE, Ltot)
    conv0 = jnp.zeros((B, M_CONV_W - 1, CONV_DIM), F32)
    ssm0 = jnp.zeros((B, M_HEADS, M_HEAD_DIM, D_STATE), F32)
    fconv0 = jnp.zeros((B, F_CONV_W - 1, 2 * D_FF), F32)
    h, mconv, ssm = _mamba_layer(x, conv0, ssm0, mamba, nb=1, L=T, Lc=min(PROMPT_CHUNK, T))
    h, fconv_a = _ffn_layer(h, fconv0, ffn[0], nb=1, L=T)
    TA = min(ATTN_TILE, Ltot)
    k, v, logf, qa, ka, vT = _proj_prompt(h.reshape(B * Ltot, D_MODEL), proj, T=TA)
    o = _attn_prompt(qa, ka, vT, T=TA)
    y, fconv_b = _ffn_layer(h, fconv0, ffn[1], nb=1, L=T, o=o.reshape(B, Ltot, D_ATTN), w_o=w_o,
                            final_w=final_w)
    return (y, ssm[None], mconv[None], jnp.stack([fconv_a, fconv_b]),
            k.reshape(B, Ltot, N_HEADS, HEAD_DIM), v.reshape(B, Ltot, N_HEADS, HEAD_DIM),
            logf.reshape(B, Ltot, N_HEADS))


def _trunk_sample(x, state_ssm, state_mconv, state_fconv, cache_k, cache_v, cache_logf, weights):
    mamba, ffn, proj, w_o, final_w = weights
    B, l, _ = x.shape
    P = cache_k.shape[1]
    h, mconv, ssm = _mamba_layer(x, state_mconv[0], state_ssm[0], mamba,
                                 nb=min(SAMPLE_SEQS_MAMBA, B), L=l, Lc=l)
    nbf = min(SAMPLE_SEQS_FFN, B)
    h, fconv_a = _ffn_layer(h, state_fconv[0], ffn[0], nb=nbf, L=l)
    k, v, logf, q = _proj_sample(h.reshape(B * l, D_MODEL), proj, T=nbf * l)
    o = _attn_sample(q.reshape(B, l, D_ATTN), k.reshape(B, l, D_ATTN), v.reshape(B, l, D_ATTN),
                     logf.reshape(B, l, N_HEADS), cache_k.reshape(B, P, D_ATTN),
                     cache_v.reshape(B, P, D_ATTN), jnp.swapaxes(cache_logf, 1, 2))
    y, fconv_b = _ffn_layer(h, state_fconv[1], ffn[1], nb=nbf, L=l, o=o, w_o=w_o, final_w=final_w)
    return (y, ssm[None], mconv[None], jnp.stack([fconv_a, fconv_b]),
            k.reshape(B, l, N_HEADS, HEAD_DIM), v.reshape(B, l, N_HEADS, HEAD_DIM),
            logf.reshape(B, l, N_HEADS))


def kernel(x_prompt, x_sample, state_ssm, state_mamba_conv, state_ffn_conv, cache_k, cache_v, cache_logf,
           norm_mix_w, norm_ffn_w, mamba_w_in, mamba_conv_w, mamba_conv_b, mamba_dt_bias, mamba_A_log,
           mamba_D, mamba_norm_w, mamba_w_out, kv_norm_w, w_k, w_v, w_f, b_f, attn_w_q, attn_w_o,
           ffn_w_up, ffn_conv_w, ffn_conv_b, ffn_w_down, final_norm_w):
    weights = _prep_weights(norm_mix_w, norm_ffn_w, mamba_w_in, mamba_conv_w, mamba_conv_b,
                            mamba_dt_bias, mamba_A_log, mamba_D, mamba_norm_w, mamba_w_out, kv_norm_w,
                            w_k, w_v, w_f, b_f, attn_w_q, attn_w_o, ffn_w_up, ffn_conv_w, ffn_conv_b,
                            ffn_w_down, final_norm_w)
    p = _trunk_prompt(x_prompt, weights)
    s = _trunk_sample(x_sample, state_ssm, state_mamba_conv, state_ffn_conv, cache_k, cache_v,
                      cache_logf, weights)
    return (p[0], s[0]) + p[1:] + s[1:]
```

```python
import functools

import jax
import jax.numpy as jnp
from jax import lax
from jax.experimental import pallas as pl
from jax.experimental.pallas import tpu as pltpu

F32 = jnp.float32
BF16 = jnp.bfloat16

EPS = 1e-6
D_MODEL = 1024
D_INNER = 2048
M_HEAD_DIM = 64
M_HEADS = 32
M_GROUPS = 4
HEADS_PER_GROUP = M_HEADS // M_GROUPS
D_STATE = 128
M_CONV_W = 4
CONV_DIM = D_INNER + 2 * M_GROUPS * D_STATE
IN_PROJ_DIM = D_INNER + CONV_DIM + M_HEADS
LANE = 128
IN_PROJ_PAD = -(-IN_PROJ_DIM // LANE) * LANE
HEAD_DIM = 64
N_HEADS = 16
D_ATTN = N_HEADS * HEAD_DIM
ATTN_SCALE = HEAD_DIM ** -0.5
LOG2E = 1.4426950408889634
D_FF = 2816
F_CONV_W = 3
HALO = 8
AUG = 2 * HEAD_DIM
VMEM_LIMIT = 56 * 1024 * 1024


def _dot(a, b):
    return jnp.dot(a, b, preferred_element_type=F32)


def _dot_nt(a, b):
    return lax.dot_general(a, b, (((1,), (1,)), ((), ())), preferred_element_type=F32)


def _dot_tn(a, b):
    return lax.dot_general(a, b, (((0,), (0,)), ((), ())), preferred_element_type=F32)


def _split3(x):
    hi = x.astype(BF16)
    r = x - hi.astype(F32)
    mid = r.astype(BF16)
    lo = (r - mid.astype(F32)).astype(BF16)
    return hi, mid, lo


def _sel_left(sel, x):
    hi, mid, lo = _split3(x)
    return _dot(sel, hi) + _dot(sel, mid) + _dot(sel, lo)


def _sel_right(x, sel):
    hi, mid, lo = _split3(x)
    return _dot(hi, sel) + _dot(mid, sel) + _dot(lo, sel)


def _sel_right_nt(sel, x):
    hi, mid, lo = _split3(x)
    return _dot_nt(sel, hi) + _dot_nt(sel, mid) + _dot_nt(sel, lo)


def _rmsnorm(x, w):
    return x * lax.rsqrt(jnp.mean(x * x, axis=-1, keepdims=True) + EPS) * w


def _silu(x):
    return x * (1.0 / (1.0 + jnp.exp(-x)))


def _softplus(x):
    return jnp.maximum(x, 0.0) + jnp.log1p(jnp.exp(-jnp.abs(x)))


def _tri(n, lower):
    r = lax.broadcasted_iota(jnp.int32, (n, n), 0)
    c = lax.broadcasted_iota(jnp.int32, (n, n), 1)
    return (r >= c) if lower else (r <= c)


def _const_spec(shape):
    nd = len(shape)
    return pl.BlockSpec(shape, lambda *_: (0,) * nd, pipeline_mode=pl.Buffered(1))


def _params(n_axes=2):
    return pltpu.CompilerParams(dimension_semantics=("arbitrary",) * n_axes,
                                vmem_limit_bytes=VMEM_LIMIT)


def _mamba_kernel(x_ref, conv0_ref, ssm0_ref, nw_ref, win_ref, wdtT_ref, cw_ref, cb_ref,
                  dtb_ref, dtbT_ref, alog_ref, alogT_ref, dx_ref, gnw_ref, wout_ref, ex_ref,
                  h_ref, convn_ref, ssmn_ref, fb_ref, y_ref, *, nb, L, Lc):
    M = nb * L
    lo = HALO - (M_CONV_W - 1)

    @pl.when(pl.program_id(1) == 0)
    def _():
        fb_ref[:, lo:HALO, :] = conv0_ref[...]
        ssmn_ref[...] = ssm0_ref[...]

    x = x_ref[...].reshape(M, D_MODEL)
    ub = _rmsnorm(x, nw_ref[...]).astype(BF16)
    zx = _dot(ub, win_ref[...])
    dtT = _softplus(_dot_nt(wdtT_ref[...], ub) + dtbT_ref[...])
    dt = _softplus(zx[:, D_INNER + CONV_DIM:IN_PROJ_DIM] + dtb_ref[...])

    fb_ref[:, HALO:HALO + L, :] = zx[:, D_INNER:D_INNER + CONV_DIM].reshape(nb, L, CONV_DIM)
    acc = cb_ref[...]
    for tap in range(M_CONV_W):
        acc = acc + fb_ref[:, lo + tap:lo + tap + L, :] * cw_ref[tap:tap + 1, :]
    halo = fb_ref[:, lo + L:HALO + L, :]
    fb_ref[:, lo:HALO, :] = halo
    convn_ref[...] = halo
    xbc = _silu(acc).reshape(M, CONV_DIM)
    xs = xbc[:, :D_INNER]
    bm = xbc[:, D_INNER:D_INNER + M_GROUPS * D_STATE].astype(BF16)
    cm = xbc[:, D_INNER + M_GROUPS * D_STATE:].astype(BF16)

    dA = dt * (-jnp.exp(alog_ref[...]))
    dAT = dtT * (-jnp.exp(alogT_ref[...]))
    tril = _tri(Lc, True)
    tril_b = tril.astype(BF16)
    triu_b = _tri(Lc, False).astype(BF16)
    ex = ex_ref[...]
    quad = lax.broadcasted_iota(jnp.int32, (Lc, 4 * M_HEAD_DIM), 1) // M_HEAD_DIM

    for s in range(nb):
        for c in range(L // Lc):
            r0 = s * L + c * Lc
            acum = _sel_left(tril_b, dA[r0:r0 + Lc])
            dtT_c = dtT[:, r0:r0 + Lc]
            acumT = _sel_right(dAT[:, r0:r0 + Lc], triu_b)
            acum_x = _sel_right(acum, ex)
            dt_x = _sel_right(dt[r0:r0 + Lc], ex)
            xs_c = xs[r0:r0 + Lc]
            xs_b = xs_c.astype(BF16)
            x_end = (xs_c * (jnp.exp(acum_x[Lc - 1:Lc, :] - acum_x) * dt_x)).astype(BF16)
            e_x = jnp.exp(acum_x)
            d_blk = jnp.exp(acumT[:, Lc - 1:Lc])
            for g in range(M_GROUPS):
                b_g = bm[r0:r0 + Lc, g * D_STATE:(g + 1) * D_STATE]
                c_g = cm[r0:r0 + Lc, g * D_STATE:(g + 1) * D_STATE]
                cb = _dot_nt(c_g, b_g)
                h0 = g * HEADS_PER_GROUP
                h_in = ssmn_ref[s, h0:h0 + HEADS_PER_GROUP].reshape(HEADS_PER_GROUP * M_HEAD_DIM, D_STATE)
                y_inter = _dot_nt(c_g, h_in.astype(BF16))
                for qd in range(HEADS_PER_GROUP // 4):
                    col = (h0 + 4 * qd) * M_HEAD_DIM
                    xq = xs_b[:, col:col + 4 * M_HEAD_DIM]
                    yq = jnp.zeros((Lc, 4 * M_HEAD_DIM), F32)
                    for j in range(4):
                        h = h0 + 4 * qd + j
                        seg = acum[:, h:h + 1] - acumT[h:h + 1, :]
                        w = cb * jnp.exp(jnp.where(tril, seg, -jnp.inf)) * dtT_c[h:h + 1, :]
                        yq = yq + _dot(w.astype(BF16), jnp.where(quad == j, xq, jnp.zeros_like(xq)))
                    y_ref[r0:r0 + Lc, col:col + 4 * M_HEAD_DIM] = (
                        yq + y_inter[:, qd * 4 * M_HEAD_DIM:(qd + 1) * 4 * M_HEAD_DIM]
                        * e_x[:, col:col + 4 * M_HEAD_DIM]
                        + dx_ref[:, col:col + 4 * M_HEAD_DIM] * xs_c[:, col:col + 4 * M_HEAD_DIM])
                st = _dot_tn(x_end[:, h0 * M_HEAD_DIM:(h0 + HEADS_PER_GROUP) * M_HEAD_DIM], b_g)
                for j in range(HEADS_PER_GROUP):
                    h = h0 + j
                    ssmn_ref[s, h] = (ssmn_ref[s, h] * d_blk[h:h + 1, :]
                                      + st[j * M_HEAD_DIM:(j + 1) * M_HEAD_DIM, :])

    yz = y_ref[...] * _silu(zx[:, :D_INNER])
    gs = D_INNER // M_GROUPS
    parts = []
    for g in range(M_GROUPS):
        blk = yz[:, g * gs:(g + 1) * gs]
        parts.append(_rmsnorm(blk, gnw_ref[:, g * gs:(g + 1) * gs]))
    yn = jnp.concatenate(parts, axis=1).astype(BF16)
    h_ref[...] = (x + _dot(yn, wout_ref[...])).reshape(nb, L, D_MODEL)


def _mamba_layer(x, conv0, ssm0, w, *, nb, L, Lc):
    B, Ltot, _ = x.shape
    grid = (B // nb, Ltot // L)
    consts = [w["norm"], w["w_in"], w["w_dtT"], w["conv_w"], w["conv_b"], w["dt_b"], w["dt_bT"],
              w["a_log"], w["a_logT"], w["d_x"], w["gn_w"], w["w_out"], w["ex"]]
    return pl.pallas_call(
        functools.partial(_mamba_kernel, nb=nb, L=L, Lc=Lc),
        grid=grid,
        in_specs=[pl.BlockSpec((nb, L, D_MODEL), lambda s, t: (s, t, 0)),
                  pl.BlockSpec((nb, M_CONV_W - 1, CONV_DIM), lambda s, t: (s, 0, 0)),
                  pl.BlockSpec((nb, M_HEADS, M_HEAD_DIM, D_STATE), lambda s, t: (s, 0, 0, 0))]
        + [_const_spec(c.shape) for c in consts],
        out_specs=[pl.BlockSpec((nb, L, D_MODEL), lambda s, t: (s, t, 0)),
                   pl.BlockSpec((nb, M_CONV_W - 1, CONV_DIM), lambda s, t: (s, 0, 0)),
                   pl.BlockSpec((nb, M_HEADS, M_HEAD_DIM, D_STATE), lambda s, t: (s, 0, 0, 0))],
        out_shape=[jax.ShapeDtypeStruct(x.shape, F32),
                   jax.ShapeDtypeStruct(conv0.shape, F32),
                   jax.ShapeDtypeStruct(ssm0.shape, F32)],
        scratch_shapes=[pltpu.VMEM((nb, HALO + L, CONV_DIM), F32),
                        pltpu.VMEM((nb * L, D_INNER), F32)],
        compiler_params=_params(),
        name="mamba_layer",
    )(x, conv0, ssm0, *consts)


def _ffn_kernel(*refs, nb, L, with_attn, final):
    it = iter(refs)
    h_ref = next(it)
    if with_attn:
        o_ref, wo_ref = next(it), next(it)
    fconv0_ref, nw_ref, wup_ref, cw_ref, cb_ref, wdown_ref = (next(it) for _ in range(6))
    if final:
        fnw_ref = next(it)
    out_ref, fconvn_ref, fb_ref = next(it), next(it), next(it)
    M = nb * L
    lo = HALO - (F_CONV_W - 1)

    @pl.when(pl.program_id(1) == 0)
    def _():
        fb_ref[:, lo:HALO, :] = fconv0_ref[...]

    h = h_ref[...].reshape(M, D_MODEL)
    if with_attn:
        h = h + _dot(o_ref[...].reshape(M, D_ATTN), wo_ref[...])
    ub = _rmsnorm(h, nw_ref[...]).astype(BF16)
    fb_ref[:, HALO:HALO + L, :] = _dot(ub, wup_ref[...]).reshape(nb, L, 2 * D_FF)
    acc = cb_ref[...]
    for tap in range(F_CONV_W):
        acc = acc + fb_ref[:, lo + tap:lo + tap + L, :] * cw_ref[tap:tap + 1, :]
    halo = fb_ref[:, lo + L:HALO + L, :]
    fb_ref[:, lo:HALO, :] = halo
    fconvn_ref[...] = halo
    a = acc.reshape(M, 2 * D_FF)
    act = (_silu(a[:, :D_FF]) * a[:, D_FF:]).astype(BF16)
    hn = h + _dot(act, wdown_ref[...])
    if final:
        hn = _rmsnorm(hn, fnw_ref[...])
    out_ref[...] = hn.reshape(nb, L, D_MODEL)


def _ffn_layer(h, fconv0, w, *, nb, L, o=None, w_o=None, final_w=None):
    B, Ltot, _ = h.shape
    grid = (B // nb, Ltot // L)
    with_attn, final = o is not None, final_w is not None
    tile = pl.BlockSpec((nb, L, D_MODEL), lambda s, t: (s, t, 0))
    args, specs = [h], [tile]
    if with_attn:
        args += [o, w_o]
        specs += [pl.BlockSpec((nb, L, D_ATTN), lambda s, t: (s, t, 0)), _const_spec(w_o.shape)]
    consts = [w["norm"], w["w_up"], w["conv_w"], w["conv_b"], w["w_down"]]
    args += [fconv0] + consts
    specs += [pl.BlockSpec((nb, F_CONV_W - 1, 2 * D_FF), lambda s, t: (s, 0, 0))]
    specs += [_const_spec(c.shape) for c in consts]
    if final:
        args.append(final_w)
        specs.append(_const_spec(final_w.shape))
    return pl.pallas_call(
        functools.partial(_ffn_kernel, nb=nb, L=L, with_attn=with_attn, final=final),
        grid=grid,
        in_specs=specs,
        out_specs=[tile, pl.BlockSpec((nb, F_CONV_W - 1, 2 * D_FF), lambda s, t: (s, 0, 0))],
        out_shape=[jax.ShapeDtypeStruct(h.shape, F32), jax.ShapeDtypeStruct(fconv0.shape, F32)],
        scratch_shapes=[pltpu.VMEM((nb, HALO + L, 2 * D_FF), F32)],
        compiler_params=_params(),
        name="conv_ffn",
    )(*args)


def _log_sigmoid(x):
    return -_softplus(-x)


def _proj_prompt_kernel(h_ref, kvw_ref, nmw_ref, wv_ref, wf_ref, bf_ref,
                        wkp_ref, wqp_ref, pk_ref, pq_ref, onek_ref, oneq_ref,
                        k_ref, v_ref, logf_ref, qa_ref, ka_ref, vT_ref, carry_ref, *, T):
    @pl.when(pl.program_id(0) == 0)
    def _():
        carry_ref[...] = jnp.zeros_like(carry_ref)

    h = h_ref[...]
    hk = _rmsnorm(h, kvw_ref[...]).astype(BF16)
    u1 = _rmsnorm(h, nmw_ref[...]).astype(BF16)
    v = _dot(hk, wv_ref[...])
    v_ref[...] = v
    vT_ref[0] = v.T.astype(BF16)
    kp = _dot(hk, wkp_ref[...])
    k_ref[...] = jnp.concatenate(
        [kp[:, j * AUG:(j + 1) * AUG] + pltpu.roll(kp[:, (j + 1) * AUG:(j + 2) * AUG], HEAD_DIM, axis=1)
         for j in range(0, N_HEADS, 2)], axis=1)
    logf3 = _log_sigmoid(_dot(hk, wf_ref[...]) + bf_ref[...])
    logf_ref[...] = logf3[:, :N_HEADS]
    c3 = carry_ref[...] + _sel_left(_tri(T, True).astype(BF16), logf3)
    carry_ref[...] = c3[T - 1:T, :]
    c_hi, c_mid, c_lo = _split3(c3 * LOG2E)
    group = lax.broadcasted_iota(jnp.int32, (T, LANE), 1) // N_HEADS
    parts = jnp.where(group == 0, c_hi, jnp.where(group == 1, c_mid,
                      jnp.where(group == 2, c_lo, jnp.zeros_like(c_lo))))
    ka_ref[...] = (kp + onek_ref[...] + _dot(parts, pk_ref[...])).astype(BF16)
    qa_ref[...] = (_dot(u1, wqp_ref[...]) + oneq_ref[...] + _dot(parts, pq_ref[...])).astype(BF16)


def _proj_prompt(h2d, w, *, T):
    Ltot = h2d.shape[0]
    consts = [w["kv_norm"], w["mix_norm"], w["w_v"], w["w_f3"], w["b_f3"],
              w["w_k_aug"], w["w_q_aug"], w["p_k"], w["p_q"], w["one_k"], w["one_q"]]
    row = lambda n: pl.BlockSpec((T, n), lambda i: (i, 0))
    return pl.pallas_call(
        functools.partial(_proj_prompt_kernel, T=T),
        grid=(Ltot // T,),
        in_specs=[row(D_MODEL)] + [_const_spec(c.shape) for c in consts],
        out_specs=[row(D_ATTN), row(D_ATTN), row(N_HEADS), row(N_HEADS * AUG), row(N_HEADS * AUG),
                   pl.BlockSpec((1, D_ATTN, T), lambda i: (i, 0, 0))],
        out_shape=[jax.ShapeDtypeStruct((Ltot, D_ATTN), F32),
                   jax.ShapeDtypeStruct((Ltot, D_ATTN), F32),
                   jax.ShapeDtypeStruct((Ltot, N_HEADS), F32),
                   jax.ShapeDtypeStruct((Ltot, N_HEADS * AUG), BF16),
                   jax.ShapeDtypeStruct((Ltot, N_HEADS * AUG), BF16),
                   jax.ShapeDtypeStruct((Ltot // T, D_ATTN, T), BF16)],
        scratch_shapes=[pltpu.VMEM((1, LANE), F32)],
        compiler_params=_params(1),
        name="kvq_proj_prompt",
    )(h2d, *consts)


def _proj_sample_kernel(h_ref, kvw_ref, nmw_ref, wk_ref, wv_ref, wf_ref, bf_ref, wq_ref,
                        k_ref, v_ref, logf_ref, q_ref):
    h = h_ref[...]
    hk = _rmsnorm(h, kvw_ref[...]).astype(BF16)
    u1 = _rmsnorm(h, nmw_ref[...]).astype(BF16)
    k_ref[...] = _dot(hk, wk_ref[...])
    v_ref[...] = _dot(hk, wv_ref[...])
    logf_ref[...] = _log_sigmoid(_dot(hk, wf_ref[...])[:, :N_HEADS] + bf_ref[...])
    q_ref[...] = _dot(u1, wq_ref[...]).astype(BF16)


def _proj_sample(h2d, w, *, T):
    n = h2d.shape[0]
    consts = [w["kv_norm"], w["mix_norm"], w["w_k"], w["w_v"], w["w_f"], w["b_f"], w["w_q"]]
    row = lambda m: pl.BlockSpec((T, m), lambda i: (i, 0))
    return pl.pallas_call(
        _proj_sample_kernel,
        grid=(n // T,),
        in_specs=[row(D_MODEL)] + [_const_spec(c.shape) for c in consts],
        out_specs=[row(D_ATTN), row(D_ATTN), row(N_HEADS), row(D_ATTN)],
        out_shape=[jax.ShapeDtypeStruct((n, D_ATTN), F32), jax.ShapeDtypeStruct((n, D_ATTN), F32),
                   jax.ShapeDtypeStruct((n, N_HEADS), F32), jax.ShapeDtypeStruct((n, D_ATTN), BF16)],
        compiler_params=_params(1),
        name="kvq_proj_sample",
    )(h2d, *consts)


def _attn_prompt_kernel(qa_ref, ka_ref, vT_ref, o_ref, s_ref, acc_ref, oT_ref, *, T, nblk):
    qi = pl.program_id(1)
    heads = range(ATTN_HEADS)
    q = [qa_ref[:, hh * AUG:(hh + 1) * AUG] for hh in heads]
    key_minus_query = (lax.broadcasted_iota(jnp.int32, (T, T), 0)
                       - lax.broadcasted_iota(jnp.int32, (T, T), 1))
    ones = jnp.ones((BF16_SUBLANES, T), BF16)

    def produce(kb, slot):
        start = pl.multiple_of(jnp.minimum(kb, nblk - 1) * T, T)
        bmax = []
        for hh in heads:
            sT = _dot_nt(ka_ref[pl.ds(start, T), hh * AUG:(hh + 1) * AUG], q[hh])
            s_ref[slot, hh] = sT
            bmax.append(jnp.max(sT, axis=0, keepdims=True))
        return tuple(bmax)

    def consume(kb, slot, ms, bmax, masked):
        out = []
        for hh in heads:
            m = ms[hh]
            sT = s_ref[slot, hh]
            if masked:
                sT = jnp.where(key_minus_query <= (qi - kb) * T, sT, -jnp.inf)
                m_new = jnp.maximum(m, jnp.max(sT, axis=0, keepdims=True))
            else:
                m_new = jnp.maximum(m, bmax[hh])
            pT = jnp.exp2(sT - m_new).astype(BF16)
            vT = vT_ref[jnp.minimum(kb, nblk - 1), hh * HEAD_DIM:(hh + 1) * HEAD_DIM, :]
            acc_ref[hh] = jnp.exp2(m - m_new) * acc_ref[hh] + _dot(jnp.concatenate([vT, ones], axis=0), pT)
            out.append(m_new)
        return tuple(out)

    def step(kb, slot, state, masked=False):
        ms, bmax = state
        bmax_next = produce(kb + 1, 1 - slot)
        return consume(kb, slot, ms, bmax, masked), bmax_next

    def pair(i, state):
        return step(2 * i + 1, 1, step(2 * i, 0, state))

    def pairs(i, state):
        for j in range(ATTN_UNROLL):
            state = pair(ATTN_UNROLL * i + j, state)
        return state

    acc_ref[...] = jnp.zeros_like(acc_ref)
    init = tuple(jnp.full((1, T), -jnp.inf, F32) for _ in heads)
    n_pairs = qi // 2
    state = lax.fori_loop(0, n_pairs // ATTN_UNROLL, pairs, (init, produce(0, 0)))
    state = lax.fori_loop(n_pairs - n_pairs % ATTN_UNROLL, n_pairs, pair, state)
    last = 2 * n_pairs
    ms, _ = step(last, 0, state, True)
    consume(last + 1, 1, ms, None, True)
    for hh in heads:
        acc = acc_ref[hh]
        oT_ref[hh * HEAD_DIM:(hh + 1) * HEAD_DIM, :] = acc[:HEAD_DIM] * (1.0 / acc[HEAD_DIM:HEAD_DIM + 1])
    o_ref[...] = oT_ref[...].T.astype(BF16)


def _attn_prompt(qa, ka, vT, *, T):
    Ltot = qa.shape[0]
    nblk = Ltot // T
    hps = ATTN_HEADS
    return pl.pallas_call(
        functools.partial(_attn_prompt_kernel, T=T, nblk=nblk),
        grid=(N_HEADS // hps, nblk),
        in_specs=[pl.BlockSpec((T, hps * AUG), lambda p, i: (i, p)),
                  pl.BlockSpec((Ltot, hps * AUG), lambda p, i: (0, p), pipeline_mode=pl.Buffered(1)),
                  pl.BlockSpec((nblk, hps * HEAD_DIM, T), lambda p, i: (0, p, 0),
                               pipeline_mode=pl.Buffered(1))],
        out_specs=pl.BlockSpec((T, hps * HEAD_DIM), lambda p, i: (i, p)),
        out_shape=jax.ShapeDtypeStruct((Ltot, D_ATTN), BF16),
        scratch_shapes=[pltpu.VMEM((2, hps, T, T), F32),
                        pltpu.VMEM((hps, HEAD_DIM + BF16_SUBLANES, T), F32),
                        pltpu.VMEM((hps * HEAD_DIM, T), F32)],
        compiler_params=_params(),
        name="fox_attn_prompt",
    )(qa, ka, vT)


def _attn_sample_kernel(q_ref, kn_ref, vn_ref, lfn_ref, ck_ref, cv_ref, lfT_ref, o_ref, *, l, P):
    HQ = 4
    R = HQ * l
    lane_p = lax.broadcasted_iota(jnp.int32, (N_HEADS, P), 1)
    suffix = lfT_ref[0]
    suffix = jnp.where(lane_p < P - 1, pltpu.roll(suffix, P - 1, axis=1), 0.0)
    d = 1
    while d < P:
        suffix = suffix + jnp.where(lane_p < P - d, pltpu.roll(suffix, P - d, axis=1), 0.0)
        d *= 2
    lfn = lfn_ref[0]
    cn = _sel_left(_tri(l, True).astype(BF16), lfn)
    cn_rows = jnp.concatenate([cn] * HQ, axis=0)
    row_head = lax.broadcasted_iota(jnp.int32, (R, N_HEADS), 0) // l
    lane_head = lax.broadcasted_iota(jnp.int32, (R, N_HEADS), 1)
    row_q = lax.broadcasted_iota(jnp.int32, (R, HQ * HEAD_DIM), 0) // l
    lane_q = lax.broadcasted_iota(jnp.int32, (R, HQ * HEAD_DIM), 1) // HEAD_DIM
    own = row_q == lane_q
    t_row = lax.broadcasted_iota(jnp.int32, (R, l), 0) % l
    j_col = lax.broadcasted_iota(jnp.int32, (R, l), 1)
    q = q_ref[0]
    outs = []
    for qd in range(N_HEADS // HQ):
        cols = slice(qd * HQ * HEAD_DIM, (qd + 1) * HQ * HEAD_DIM)
        sel = lane_head == row_head + qd * HQ
        qx = jnp.concatenate([q[:, cols]] * HQ, axis=0)
        qx = jnp.where(own, qx, jnp.zeros_like(qx))
        cn_col = jnp.sum(jnp.where(sel, cn_rows, 0.0), axis=1, keepdims=True)
        cn_keys = _sel_right_nt(sel.astype(BF16), cn)
        past_bias = jnp.concatenate(
            [jnp.broadcast_to(suffix[qd * HQ + j:qd * HQ + j + 1, :], (l, P)) for j in range(HQ)], axis=0)
        s_p = _dot_nt(qx, ck_ref[0, :, cols]) + past_bias + cn_col
        s_n = _dot_nt(qx, kn_ref[0, :, cols].astype(BF16)) + cn_col - cn_keys
        s_n = jnp.where(j_col <= t_row, s_n, -jnp.inf)
        m = jnp.maximum(jnp.max(s_p, axis=1, keepdims=True), jnp.max(s_n, axis=1, keepdims=True))
        p_p = jnp.exp(s_p - m)
        p_n = jnp.exp(s_n - m)
        den = jnp.sum(p_p, axis=1, keepdims=True) + jnp.sum(p_n, axis=1, keepdims=True)
        o = (_dot(p_p.astype(BF16), cv_ref[0, :, cols])
             + _dot(p_n.astype(BF16), vn_ref[0, :, cols].astype(BF16))) * (1.0 / den)
        o = jnp.where(own, o, 0.0)
        acc = o[0:l]
        for j in range(1, HQ):
            acc = acc + o[j * l:(j + 1) * l]
        outs.append(acc)
    o_ref[0] = jnp.concatenate(outs, axis=1).astype(BF16)


def _attn_sample(q, kn, vn, lfn, cache_k, cache_v, cache_lfT):
    B, l, _ = q.shape
    P = cache_k.shape[1]
    seq = lambda a, b: pl.BlockSpec((1, a, b), lambda i: (i, 0, 0))
    return pl.pallas_call(
        functools.partial(_attn_sample_kernel, l=l, P=P),
        grid=(B,),
        in_specs=[seq(l, D_ATTN), seq(l, D_ATTN), seq(l, D_ATTN), seq(l, N_HEADS),
                  seq(P, D_ATTN), seq(P, D_ATTN), seq(N_HEADS, P)],
        out_specs=seq(l, D_ATTN),
        out_shape=jax.ShapeDtypeStruct((B, l, D_ATTN), BF16),
        compiler_params=_params(1),
        name="fox_attn_sample",
    )(q, kn, vn, lfn, cache_k, cache_v, cache_lfT)


def _row(v):
    return v.reshape(1, -1).astype(F32)


def _col(v):
    return v.reshape(-1, 1).astype(F32)


def _prep_weights(norm_mix_w, norm_ffn_w, mamba_w_in, mamba_conv_w, mamba_conv_b, mamba_dt_bias,
                  mamba_A_log, mamba_D, mamba_norm_w, mamba_w_out, kv_norm_w, w_k, w_v, w_f, b_f,
                  attn_w_q, attn_w_o, ffn_w_up, ffn_conv_w, ffn_conv_b, ffn_w_down, final_norm_w):
    w_in = mamba_w_in[0]
    mamba = {
        "norm": _row(norm_mix_w[0]),
        "w_in": jnp.pad(w_in, ((0, 0), (0, IN_PROJ_PAD - IN_PROJ_DIM))).astype(BF16),
        "w_dtT": w_in[:, D_INNER + CONV_DIM:].T.astype(BF16),
        "conv_w": mamba_conv_w[0], "conv_b": _row(mamba_conv_b[0]),
        "dt_b": _row(mamba_dt_bias[0]), "dt_bT": _col(mamba_dt_bias[0]),
        "a_log": _row(mamba_A_log[0]), "a_logT": _col(mamba_A_log[0]),
        "d_x": _row(jnp.repeat(mamba_D[0], M_HEAD_DIM)),
        "gn_w": _row(mamba_norm_w[0]),
        "w_out": mamba_w_out[0].astype(BF16),
        "ex": jnp.repeat(jnp.eye(M_HEADS, dtype=BF16), M_HEAD_DIM, axis=1),
    }
    ffn = [{"norm": _row(norm_ffn_w[i]), "w_up": ffn_w_up[i].astype(BF16), "conv_w": ffn_conv_w[i],
            "conv_b": _row(ffn_conv_b[i]), "w_down": ffn_w_down[i].astype(BF16)} for i in range(2)]

    def per_head_pad(wm):
        wm = wm.reshape(D_MODEL, N_HEADS, HEAD_DIM)
        return jnp.pad(wm, ((0, 0), (0, 0), (0, AUG - HEAD_DIM))).reshape(D_MODEL, N_HEADS * AUG)

    def place(offset, value):
        m = jnp.zeros((LANE // N_HEADS, N_HEADS, N_HEADS, AUG), F32)
        for i in range(3):
            m = m.at[i, :, :, offset + i].set(value * jnp.eye(N_HEADS, dtype=F32))
        return m.reshape(LANE, N_HEADS * AUG).astype(BF16)

    def three_groups(a):
        return jnp.pad(jnp.concatenate([a] * 3, axis=-1), ((0, 0), (0, LANE - 3 * N_HEADS)))

    def ones_at(offset):
        v = jnp.zeros((N_HEADS, AUG), F32).at[:, offset:offset + 3].set(1.0)
        return v.reshape(1, N_HEADS * AUG)

    w_q = attn_w_q[0] * ATTN_SCALE
    proj = {
        "kv_norm": _row(kv_norm_w), "mix_norm": _row(norm_mix_w[1]),
        "w_k": w_k.astype(BF16), "w_v": w_v.astype(BF16),
        "w_f": jnp.pad(w_f, ((0, 0), (0, LANE - N_HEADS))).astype(BF16), "b_f": _row(b_f),
        "w_f3": three_groups(w_f).astype(BF16), "b_f3": three_groups(_row(b_f)),
        "w_q": w_q.astype(BF16),
        "w_k_aug": per_head_pad(w_k).astype(BF16), "w_q_aug": per_head_pad(w_q * LOG2E).astype(BF16),
        "p_k": place(HEAD_DIM, -1.0), "p_q": place(HEAD_DIM + 3, 1.0),
        "one_k": ones_at(HEAD_DIM + 3), "one_q": ones_at(HEAD_DIM),
    }
    return mamba, ffn, proj, attn_w_o[0].astype(BF16), _row(final_norm_w)


PROMPT_TILE = 256
PROMPT_CHUNK = 128
ATTN_TILE = 256
ATTN_HEADS = 4
ATTN_UNROLL = 4
BF16_SUBLANES = 16
SAMPLE_SEQS_MAMBA = 4
SAMPLE_SEQS_FFN = 16


def _trunk_prompt(x, weights):
    mamba, ffn, proj, w_o, final_w = weights
    B, Ltot, _ = x.shape
    T = min(PROMPT_TILE, Ltot)
    conv0 = jnp.zeros((B, M_CONV_W - 1, CONV_DIM), F32)
    ssm0 = jnp.zeros((B, M_HEADS, M_HEAD_DIM, D_STATE), F32)
    fconv0 = jnp.zeros((B, F_CONV_W - 1, 2 * D_FF), F32)
    h, mconv, ssm = _mamba_layer(x, conv0, ssm0, mamba, nb=1, L=T, Lc=min(PROMPT_CHUNK, T))
    h, fconv_a = _ffn_layer(h, fconv0, ffn[0], nb=1, L=T)
    TA = min(ATTN_TILE, Ltot)
    k, v, logf, qa, ka, vT = _proj_prompt(h.reshape(B * Ltot, D_MODEL), proj, T=TA)
    o = _attn_prompt(qa, ka, vT, T=TA)
    y, fconv_b = _ffn_layer(h, fconv0, ffn[1], nb=1, L=T, o=o.reshape(B, Ltot, D_ATTN), w_o=w_o,
                            final_w=final_w)
    return (y, ssm[None], mconv[None], jnp.stack([fconv_a, fconv_b]),
            k.reshape(B, Ltot, N_HEADS, HEAD_DIM), v.reshape(B, Ltot, N_HEADS, HEAD_DIM),
            logf.reshape(B, Ltot, N_HEADS))


def _trunk_sample(x, state_ssm, state_mconv, state_fconv, cache_k, cache_v, cache_logf, weights):
    mamba, ffn, proj, w_o, final_w = weights
    B, l, _ = x.shape
    P = cache_k.shape[1]
    h, mconv, ssm = _mamba_layer(x, state_mconv[0], state_ssm[0], mamba,
                                 nb=min(SAMPLE_SEQS_MAMBA, B), L=l, Lc=l)
    nbf = min(SAMPLE_SEQS_FFN, B)
    h, fconv_a = _ffn_layer(h, state_fconv[0], ffn[0], nb=nbf, L=l)
    k, v, logf, q = _proj_sample(h.reshape(B * l, D_MODEL), proj, T=nbf * l)
    o = _attn_sample(q.reshape(B, l, D_ATTN), k.reshape(B, l, D_ATTN), v.reshape(B, l, D_ATTN),
                     logf.reshape(B, l, N_HEADS), cache_k.reshape(B, P, D_ATTN).astype(BF16),
                     cache_v.reshape(B, P, D_ATTN).astype(BF16), jnp.swapaxes(cache_logf, 1, 2))
    y, fconv_b = _ffn_layer(h, state_fconv[1], ffn[1], nb=nbf, L=l, o=o, w_o=w_o, final_w=final_w)
    return (y, ssm[None], mconv[None], jnp.stack([fconv_a, fconv_b]),
            k.reshape(B, l, N_HEADS, HEAD_DIM), v.reshape(B, l, N_HEADS, HEAD_DIM),
            logf.reshape(B, l, N_HEADS))


def kernel(x_prompt, x_sample, state_ssm, state_mamba_conv, state_ffn_conv, cache_k, cache_v, cache_logf,
           norm_mix_w, norm_ffn_w, mamba_w_in, mamba_conv_w, mamba_conv_b, mamba_dt_bias, mamba_A_log,
           mamba_D, mamba_norm_w, mamba_w_out, kv_norm_w, w_k, w_v, w_f, b_f, attn_w_q, attn_w_o,
           ffn_w_up, ffn_conv_w, ffn_conv_b, ffn_w_down, final_norm_w):
    weights = _prep_weights(norm_mix_w, norm_ffn_w, mamba_w_in, mamba_conv_w, mamba_conv_b,
                            mamba_dt_bias, mamba_A_log, mamba_D, mamba_norm_w, mamba_w_out, kv_norm_w,
                            w_k, w_v, w_f, b_f, attn_w_q, attn_w_o, ffn_w_up, ffn_conv_w, ffn_conv_b,
                            ffn_w_down, final_norm_w)
    p = _trunk_prompt(x_prompt, weights)
    s = _trunk_sample(x_sample, state_ssm, state_mamba_conv, state_ffn_conv, cache_k, cache_v,
                      cache_logf, weights)
    return (p[0], s[0]) + p[1:] + s[1:]
```

```python
import functools

import jax
import jax.numpy as jnp
from jax import lax
from jax.experimental import pallas as pl
from jax.experimental.pallas import tpu as pltpu

F32 = jnp.float32
BF16 = jnp.bfloat16

EPS = 1e-6
D_MODEL = 1024
D_INNER = 2048
M_HEAD_DIM = 64
M_HEADS = 32
M_GROUPS = 4
HEADS_PER_GROUP = M_HEADS // M_GROUPS
D_STATE = 128
M_CONV_W = 4
CONV_DIM = D_INNER + 2 * M_GROUPS * D_STATE
IN_PROJ_DIM = D_INNER + CONV_DIM + M_HEADS
LANE = 128
IN_PROJ_PAD = -(-IN_PROJ_DIM // LANE) * LANE
HEAD_DIM = 64
N_HEADS = 16
D_ATTN = N_HEADS * HEAD_DIM
ATTN_SCALE = HEAD_DIM ** -0.5
LOG2E = 1.4426950408889634
D_FF = 2816
F_CONV_W = 3
HALO = 8
AUG = 2 * HEAD_DIM
VMEM_LIMIT = 56 * 1024 * 1024


def _dot(a, b):
    return jnp.dot(a, b, preferred_element_type=F32)


def _dot_nt(a, b):
    return lax.dot_general(a, b, (((1,), (1,)), ((), ())), preferred_element_type=F32)


def _dot_tn(a, b):
    return lax.dot_general(a, b, (((0,), (0,)), ((), ())), preferred_element_type=F32)


def _split3(x):
    hi = x.astype(BF16)
    r = x - hi.astype(F32)
    mid = r.astype(BF16)
    lo = (r - mid.astype(F32)).astype(BF16)
    return hi, mid, lo


def _sel_left(sel, x):
    hi, mid, lo = _split3(x)
    return _dot(sel, hi) + _dot(sel, mid) + _dot(sel, lo)


def _sel_right(x, sel):
    hi, mid, lo = _split3(x)
    return _dot(hi, sel) + _dot(mid, sel) + _dot(lo, sel)


def _sel_right_nt(sel, x):
    hi, mid, lo = _split3(x)
    return _dot_nt(sel, hi) + _dot_nt(sel, mid) + _dot_nt(sel, lo)


def _rmsnorm(x, w):
    return x * lax.rsqrt(jnp.mean(x * x, axis=-1, keepdims=True) + EPS) * w


def _silu(x):
    return x * (1.0 / (1.0 + jnp.exp(-x)))


def _softplus(x):
    return jnp.maximum(x, 0.0) + jnp.log1p(jnp.exp(-jnp.abs(x)))


def _tri(n, lower):
    r = lax.broadcasted_iota(jnp.int32, (n, n), 0)
    c = lax.broadcasted_iota(jnp.int32, (n, n), 1)
    return (r >= c) if lower else (r <= c)


def _const_spec(shape):
    nd = len(shape)
    return pl.BlockSpec(shape, lambda *_: (0,) * nd, pipeline_mode=pl.Buffered(1))


def _params(n_axes=2):
    return pltpu.CompilerParams(dimension_semantics=("arbitrary",) * n_axes,
                                vmem_limit_bytes=VMEM_LIMIT)


def _mamba_kernel(x_ref, conv0_ref, ssm0_ref, nw_ref, win_ref, wdtT_ref, cw_ref, cb_ref,
                  dtb_ref, dtbT_ref, alog_ref, alogT_ref, dx_ref, gnw_ref, wout_ref, ex_ref,
                  h_ref, convn_ref, ssmn_ref, fb_ref, y_ref, *, nb, L, Lc):
    M = nb * L
    lo = HALO - (M_CONV_W - 1)

    @pl.when(pl.program_id(1) == 0)
    def _():
        fb_ref[:, lo:HALO, :] = conv0_ref[...]
        ssmn_ref[...] = ssm0_ref[...]

    x = x_ref[...].reshape(M, D_MODEL)
    ub = _rmsnorm(x, nw_ref[...]).astype(BF16)
    zx = _dot(ub, win_ref[...])
    dtT = _softplus(_dot_nt(wdtT_ref[...], ub) + dtbT_ref[...])
    dt = _softplus(zx[:, D_INNER + CONV_DIM:IN_PROJ_DIM] + dtb_ref[...])

    fb_ref[:, HALO:HALO + L, :] = zx[:, D_INNER:D_INNER + CONV_DIM].reshape(nb, L, CONV_DIM)
    acc = cb_ref[...]
    for tap in range(M_CONV_W):
        acc = acc + fb_ref[:, lo + tap:lo + tap + L, :] * cw_ref[tap:tap + 1, :]
    halo = fb_ref[:, lo + L:HALO + L, :]
    fb_ref[:, lo:HALO, :] = halo
    convn_ref[...] = halo
    xbc = _silu(acc).reshape(M, CONV_DIM)
    xs = xbc[:, :D_INNER]
    bm = xbc[:, D_INNER:D_INNER + M_GROUPS * D_STATE].astype(BF16)
    cm = xbc[:, D_INNER + M_GROUPS * D_STATE:].astype(BF16)

    dA = dt * (-jnp.exp(alog_ref[...]))
    dAT = dtT * (-jnp.exp(alogT_ref[...]))
    tril = _tri(Lc, True)
    tril_b = tril.astype(BF16)
    triu_b = _tri(Lc, False).astype(BF16)
    ex = ex_ref[...]
    quad = lax.broadcasted_iota(jnp.int32, (Lc, 4 * M_HEAD_DIM), 1) // M_HEAD_DIM

    for s in range(nb):
        for c in range(L // Lc):
            r0 = s * L + c * Lc
            acum = _sel_left(tril_b, dA[r0:r0 + Lc])
            dtT_c = dtT[:, r0:r0 + Lc]
            acumT = _sel_right(dAT[:, r0:r0 + Lc], triu_b)
            acum_x = _sel_right(acum, ex)
            dt_x = _sel_right(dt[r0:r0 + Lc], ex)
            xs_c = xs[r0:r0 + Lc]
            xs_b = xs_c.astype(BF16)
            x_end = (xs_c * (jnp.exp(acum_x[Lc - 1:Lc, :] - acum_x) * dt_x)).astype(BF16)
            e_x = jnp.exp(acum_x)
            d_blk = jnp.exp(acumT[:, Lc - 1:Lc])
            for g in range(M_GROUPS):
                b_g = bm[r0:r0 + Lc, g * D_STATE:(g + 1) * D_STATE]
                c_g = cm[r0:r0 + Lc, g * D_STATE:(g + 1) * D_STATE]
                cb = _dot_nt(c_g, b_g)
                h0 = g * HEADS_PER_GROUP
                h_in = ssmn_ref[s, h0:h0 + HEADS_PER_GROUP].reshape(HEADS_PER_GROUP * M_HEAD_DIM, D_STATE)
                y_inter = _dot_nt(c_g, h_in.astype(BF16))
                for qd in range(HEADS_PER_GROUP // 4):
                    col = (h0 + 4 * qd) * M_HEAD_DIM
                    xq = xs_b[:, col:col + 4 * M_HEAD_DIM]
                    yq = jnp.zeros((Lc, 4 * M_HEAD_DIM), F32)
                    for j in range(4):
                        h = h0 + 4 * qd + j
                        seg = acum[:, h:h + 1] - acumT[h:h + 1, :]
                        w = cb * jnp.exp(jnp.where(tril, seg, -jnp.inf)) * dtT_c[h:h + 1, :]
                        yq = yq + _dot(w.astype(BF16), jnp.where(quad == j, xq, jnp.zeros_like(xq)))
                    y_ref[r0:r0 + Lc, col:col + 4 * M_HEAD_DIM] = (
                        yq + y_inter[:, qd * 4 * M_HEAD_DIM:(qd + 1) * 4 * M_HEAD_DIM]
                        * e_x[:, col:col + 4 * M_HEAD_DIM]
                        + dx_ref[:, col:col + 4 * M_HEAD_DIM] * xs_c[:, col:col + 4 * M_HEAD_DIM])
                st = _dot_tn(x_end[:, h0 * M_HEAD_DIM:(h0 + HEADS_PER_GROUP) * M_HEAD_DIM], b_g)
                for j in range(HEADS_PER_GROUP):
                    h = h0 + j
                    ssmn_ref[s, h] = (ssmn_ref[s, h] * d_blk[h:h + 1, :]
                                      + st[j * M_HEAD_DIM:(j + 1) * M_HEAD_DIM, :])

    yz = y_ref[...] * _silu(zx[:, :D_INNER])
    gs = D_INNER // M_GROUPS
    parts = []
    for g in range(M_GROUPS):
        blk = yz[:, g * gs:(g + 1) * gs]
        parts.append(_rmsnorm(blk, gnw_ref[:, g * gs:(g + 1) * gs]))
    yn = jnp.concatenate(parts, axis=1).astype(BF16)
    h_ref[...] = (x + _dot(yn, wout_ref[...])).reshape(nb, L, D_MODEL)


def _mamba_layer(x, conv0, ssm0, w, *, nb, L, Lc):
    B, Ltot, _ = x.shape
    grid = (B // nb, Ltot // L)
    consts = [w["norm"], w["w_in"], w["w_dtT"], w["conv_w"], w["conv_b"], w["dt_b"], w["dt_bT"],
              w["a_log"], w["a_logT"], w["d_x"], w["gn_w"], w["w_out"], w["ex"]]
    return pl.pallas_call(
        functools.partial(_mamba_kernel, nb=nb, L=L, Lc=Lc),
        grid=grid,
        in_specs=[pl.BlockSpec((nb, L, D_MODEL), lambda s, t: (s, t, 0)),
                  pl.BlockSpec((nb, M_CONV_W - 1, CONV_DIM), lambda s, t: (s, 0, 0)),
                  pl.BlockSpec((nb, M_HEADS, M_HEAD_DIM, D_STATE), lambda s, t: (s, 0, 0, 0))]
        + [_const_spec(c.shape) for c in consts],
        out_specs=[pl.BlockSpec((nb, L, D_MODEL), lambda s, t: (s, t, 0)),
                   pl.BlockSpec((nb, M_CONV_W - 1, CONV_DIM), lambda s, t: (s, 0, 0)),
                   pl.BlockSpec((nb, M_HEADS, M_HEAD_DIM, D_STATE), lambda s, t: (s, 0, 0, 0))],
        out_shape=[jax.ShapeDtypeStruct(x.shape, F32),
                   jax.ShapeDtypeStruct(conv0.shape, F32),
                   jax.ShapeDtypeStruct(ssm0.shape, F32)],
        scratch_shapes=[pltpu.VMEM((nb, HALO + L, CONV_DIM), F32),
                        pltpu.VMEM((nb * L, D_INNER), F32)],
        compiler_params=_params(),
        name="mamba_layer",
    )(x, conv0, ssm0, *consts)


def _ffn_kernel(*refs, nb, L, with_attn, final):
    it = iter(refs)
    h_ref = next(it)
    if with_attn:
        o_ref, wo_ref = next(it), next(it)
    fconv0_ref, nw_ref, wup_ref, cw_ref, cb_ref, wdown_ref = (next(it) for _ in range(6))
    if final:
        fnw_ref = next(it)
    out_ref, fconvn_ref, fb_ref = next(it), next(it), next(it)
    M = nb * L
    lo = HALO - (F_CONV_W - 1)

    @pl.when(pl.program_id(1) == 0)
    def _():
        fb_ref[:, lo:HALO, :] = fconv0_ref[...]

    h = h_ref[...].reshape(M, D_MODEL)
    if with_attn:
        h = h + _dot(o_ref[...].reshape(M, D_ATTN), wo_ref[...])
    ub = _rmsnorm(h, nw_ref[...]).astype(BF16)
    fb_ref[:, HALO:HALO + L, :] = _dot(ub, wup_ref[...]).reshape(nb, L, 2 * D_FF)
    acc = cb_ref[...]
    for tap in range(F_CONV_W):
        acc = acc + fb_ref[:, lo + tap:lo + tap + L, :] * cw_ref[tap:tap + 1, :]
    halo = fb_ref[:, lo + L:HALO + L, :]
    fb_ref[:, lo:HALO, :] = halo
    fconvn_ref[...] = halo
    a = acc.reshape(M, 2 * D_FF)
    act = (_silu(a[:, :D_FF]) * a[:, D_FF:]).astype(BF16)
    hn = h + _dot(act, wdown_ref[...])
    if final:
        hn = _rmsnorm(hn, fnw_ref[...])
    out_ref[...] = hn.reshape(nb, L, D_MODEL)


def _ffn_layer(h, fconv0, w, *, nb, L, o=None, w_o=None, final_w=None):
    B, Ltot, _ = h.shape
    grid = (B // nb, Ltot // L)
    with_attn, final = o is not None, final_w is not None
    tile = pl.BlockSpec((nb, L, D_MODEL), lambda s, t: (s, t, 0))
    args, specs = [h], [tile]
    if with_attn:
        args += [o, w_o]
        specs += [pl.BlockSpec((nb, L, D_ATTN), lambda s, t: (s, t, 0)), _const_spec(w_o.shape)]
    consts = [w["norm"], w["w_up"], w["conv_w"], w["conv_b"], w["w_down"]]
    args += [fconv0] + consts
    specs += [pl.BlockSpec((nb, F_CONV_W - 1, 2 * D_FF), lambda s, t: (s, 0, 0))]
    specs += [_const_spec(c.shape) for c in consts]
    if final:
        args.append(final_w)
        specs.append(_const_spec(final_w.shape))
    return pl.pallas_call(
        functools.partial(_ffn_kernel, nb=nb, L=L, with_attn=with_attn, final=final),
        grid=grid,
        in_specs=specs,
        out_specs=[tile, pl.BlockSpec((nb, F_CONV_W - 1, 2 * D_FF), lambda s, t: (s, 0, 0))],
        out_shape=[jax.ShapeDtypeStruct(h.shape, F32), jax.ShapeDtypeStruct(fconv0.shape, F32)],
        scratch_shapes=[pltpu.VMEM((nb, HALO + L, 2 * D_FF), F32)],
        compiler_params=_params(),
        name="conv_ffn",
    )(*args)


def _log_sigmoid(x):
    return -_softplus(-x)


def _proj_prompt_kernel(h_ref, kvw_ref, nmw_ref, wv_ref, wf_ref, bf_ref,
                        wk_ref, wq_ref, pk_ref, pq_ref, onek_ref, oneq_ref,
                        k_ref, v_ref, logf_ref, qa_ref, ka_ref, vT_ref, carry_ref, *, T):
    @pl.when(pl.program_id(0) == 0)
    def _():
        carry_ref[...] = jnp.zeros_like(carry_ref)

    h = h_ref[...]
    hk = _rmsnorm(h, kvw_ref[...]).astype(BF16)
    u1 = _rmsnorm(h, nmw_ref[...]).astype(BF16)
    v = _dot(hk, wv_ref[...])
    v_ref[...] = v
    vT_ref[0] = v.T.astype(BF16)
    k = _dot(hk, wk_ref[...])
    k_ref[...] = k
    low = lax.broadcasted_iota(jnp.int32, (T, AUG), 1) < HEAD_DIM

    def spread(a):
        slots = []
        for j in range(N_HEADS // 2):
            two = a[:, j * AUG:(j + 1) * AUG]
            slots += [jnp.where(low, two, 0.0), jnp.where(low, pltpu.roll(two, HEAD_DIM, axis=1), 0.0)]
        return jnp.concatenate(slots, axis=1)
    logf3 = _log_sigmoid(_dot(hk, wf_ref[...]) + bf_ref[...])
    logf_ref[...] = logf3[:, :N_HEADS]
    c3 = carry_ref[...] + _sel_left(_tri(T, True).astype(BF16), logf3)
    carry_ref[...] = c3[T - 1:T, :]
    c_hi, c_mid, c_lo = _split3(c3 * LOG2E)
    group = lax.broadcasted_iota(jnp.int32, (T, LANE), 1) // N_HEADS
    parts = jnp.where(group == 0, c_hi, jnp.where(group == 1, c_mid,
                      jnp.where(group == 2, c_lo, jnp.zeros_like(c_lo))))
    ka_ref[...] = (spread(k) + onek_ref[...] + _dot(parts, pk_ref[...])).astype(BF16)
    qa_ref[...] = (spread(_dot(u1, wq_ref[...])) + oneq_ref[...] + _dot(parts, pq_ref[...])).astype(BF16)


def _proj_prompt(h2d, w, *, T):
    Ltot = h2d.shape[0]
    consts = [w["kv_norm"], w["mix_norm"], w["w_v"], w["w_f3"], w["b_f3"],
              w["w_k"], w["w_q_log2"], w["p_k"], w["p_q"], w["one_k"], w["one_q"]]
    row = lambda n: pl.BlockSpec((T, n), lambda i: (i, 0))
    return pl.pallas_call(
        functools.partial(_proj_prompt_kernel, T=T),
        grid=(Ltot // T,),
        in_specs=[row(D_MODEL)] + [_const_spec(c.shape) for c in consts],
        out_specs=[row(D_ATTN), row(D_ATTN), row(N_HEADS), row(N_HEADS * AUG), row(N_HEADS * AUG),
                   pl.BlockSpec((1, D_ATTN, T), lambda i: (i, 0, 0))],
        out_shape=[jax.ShapeDtypeStruct((Ltot, D_ATTN), F32),
                   jax.ShapeDtypeStruct((Ltot, D_ATTN), F32),
                   jax.ShapeDtypeStruct((Ltot, N_HEADS), F32),
                   jax.ShapeDtypeStruct((Ltot, N_HEADS * AUG), BF16),
                   jax.ShapeDtypeStruct((Ltot, N_HEADS * AUG), BF16),
                   jax.ShapeDtypeStruct((Ltot // T, D_ATTN, T), BF16)],
        scratch_shapes=[pltpu.VMEM((1, LANE), F32)],
        compiler_params=_params(1),
        name="kvq_proj_prompt",
    )(h2d, *consts)


def _proj_sample_kernel(h_ref, kvw_ref, nmw_ref, wk_ref, wv_ref, wf_ref, bf_ref, wq_ref,
                        k_ref, v_ref, logf_ref, q_ref):
    h = h_ref[...]
    hk = _rmsnorm(h, kvw_ref[...]).astype(BF16)
    u1 = _rmsnorm(h, nmw_ref[...]).astype(BF16)
    k_ref[...] = _dot(hk, wk_ref[...])
    v_ref[...] = _dot(hk, wv_ref[...])
    logf_ref[...] = _log_sigmoid(_dot(hk, wf_ref[...])[:, :N_HEADS] + bf_ref[...])
    q_ref[...] = _dot(u1, wq_ref[...]).astype(BF16)


def _proj_sample(h2d, w, *, T):
    n = h2d.shape[0]
    consts = [w["kv_norm"], w["mix_norm"], w["w_k"], w["w_v"], w["w_f"], w["b_f"], w["w_q"]]
    row = lambda m: pl.BlockSpec((T, m), lambda i: (i, 0))
    return pl.pallas_call(
        _proj_sample_kernel,
        grid=(n // T,),
        in_specs=[row(D_MODEL)] + [_const_spec(c.shape) for c in consts],
        out_specs=[row(D_ATTN), row(D_ATTN), row(N_HEADS), row(D_ATTN)],
        out_shape=[jax.ShapeDtypeStruct((n, D_ATTN), F32), jax.ShapeDtypeStruct((n, D_ATTN), F32),
                   jax.ShapeDtypeStruct((n, N_HEADS), F32), jax.ShapeDtypeStruct((n, D_ATTN), BF16)],
        compiler_params=_params(1),
        name="kvq_proj_sample",
    )(h2d, *consts)


def _attn_prompt_kernel(qa_ref, ka_ref, vT_ref, o_ref, s_ref, acc_ref, oT_ref, *, T, nblk):
    qi = pl.program_id(1)
    heads = range(ATTN_HEADS)
    q = [qa_ref[:, hh * AUG:(hh + 1) * AUG] for hh in heads]
    key_minus_query = (lax.broadcasted_iota(jnp.int32, (T, T), 0)
                       - lax.broadcasted_iota(jnp.int32, (T, T), 1))
    ones = jnp.ones((BF16_SUBLANES, T), BF16)

    def produce(kb, slot):
        start = pl.multiple_of(jnp.minimum(kb, nblk - 1) * T, T)
        bmax = []
        for hh in heads:
            sT = _dot_nt(ka_ref[pl.ds(start, T), hh * AUG:(hh + 1) * AUG], q[hh])
            s_ref[slot, hh] = sT
            bmax.append(jnp.max(sT, axis=0, keepdims=True))
        return tuple(bmax)

    def consume(kb, slot, ms, bmax, masked):
        out = []
        for hh in heads:
            m = ms[hh]
            sT = s_ref[slot, hh]
            if masked:
                sT = jnp.where(key_minus_query <= (qi - kb) * T, sT, -jnp.inf)
                m_new = jnp.maximum(m, jnp.max(sT, axis=0, keepdims=True))
            else:
                m_new = jnp.maximum(m, bmax[hh])
            pT = jnp.exp2(sT - m_new).astype(BF16)
            vT = vT_ref[jnp.minimum(kb, nblk - 1), hh * HEAD_DIM:(hh + 1) * HEAD_DIM, :]
            acc_ref[hh] = jnp.exp2(m - m_new) * acc_ref[hh] + _dot(jnp.concatenate([vT, ones], axis=0), pT)
            out.append(m_new)
        return tuple(out)

    def step(kb, slot, state, masked=False):
        ms, bmax = state
        bmax_next = produce(kb + 1, 1 - slot)
        return consume(kb, slot, ms, bmax, masked), bmax_next

    def pair(i, state):
        return step(2 * i + 1, 1, step(2 * i, 0, state))

    def pairs(i, state):
        for j in range(ATTN_UNROLL):
            state = pair(ATTN_UNROLL * i + j, state)
        return state

    acc_ref[...] = jnp.zeros_like(acc_ref)
    init = tuple(jnp.full((1, T), -jnp.inf, F32) for _ in heads)
    n_pairs = qi // 2
    state = lax.fori_loop(0, n_pairs // ATTN_UNROLL, pairs, (init, produce(0, 0)))
    state = lax.fori_loop(n_pairs - n_pairs % ATTN_UNROLL, n_pairs, pair, state)
    last = 2 * n_pairs
    ms, _ = step(last, 0, state, True)
    consume(last + 1, 1, ms, None, True)
    for hh in heads:
        acc = acc_ref[hh]
        oT_ref[hh * HEAD_DIM:(hh + 1) * HEAD_DIM, :] = acc[:HEAD_DIM] * (1.0 / acc[HEAD_DIM:HEAD_DIM + 1])
    o_ref[...] = oT_ref[...].T.astype(BF16)


def _attn_prompt(qa, ka, vT, *, T):
    Ltot = qa.shape[0]
    nblk = Ltot // T
    hps = ATTN_HEADS
    return pl.pallas_call(
        functools.partial(_attn_prompt_kernel, T=T, nblk=nblk),
        grid=(N_HEADS // hps, nblk),
        in_specs=[pl.BlockSpec((T, hps * AUG), lambda p, i: (i, p)),
                  pl.BlockSpec((Ltot, hps * AUG), lambda p, i: (0, p), pipeline_mode=pl.Buffered(1)),
                  pl.BlockSpec((nblk, hps * HEAD_DIM, T), lambda p, i: (0, p, 0),
                               pipeline_mode=pl.Buffered(1))],
        out_specs=pl.BlockSpec((T, hps * HEAD_DIM), lambda p, i: (i, p)),
        out_shape=jax.ShapeDtypeStruct((Ltot, D_ATTN), BF16),
        scratch_shapes=[pltpu.VMEM((2, hps, T, T), F32),
                        pltpu.VMEM((hps, HEAD_DIM + BF16_SUBLANES, T), F32),
                        pltpu.VMEM((hps * HEAD_DIM, T), F32)],
        compiler_params=_params(),
        name="fox_attn_prompt",
    )(qa, ka, vT)


def _attn_sample_kernel(q_ref, kn_ref, vn_ref, lfn_ref, ck_ref, cv_ref, lfT_ref, o_ref, *, l, P):
    HQ = 4
    R = HQ * l
    lane_p = lax.broadcasted_iota(jnp.int32, (N_HEADS, P), 1)
    suffix = lfT_ref[0]
    suffix = jnp.where(lane_p < P - 1, pltpu.roll(suffix, P - 1, axis=1), 0.0)
    d = 1
    while d < P:
        suffix = suffix + jnp.where(lane_p < P - d, pltpu.roll(suffix, P - d, axis=1), 0.0)
        d *= 2
    lfn = lfn_ref[0]
    cn = _sel_left(_tri(l, True).astype(BF16), lfn)
    cn_rows = jnp.concatenate([cn] * HQ, axis=0)
    row_head = lax.broadcasted_iota(jnp.int32, (R, N_HEADS), 0) // l
    lane_head = lax.broadcasted_iota(jnp.int32, (R, N_HEADS), 1)
    row_q = lax.broadcasted_iota(jnp.int32, (R, HQ * HEAD_DIM), 0) // l
    lane_q = lax.broadcasted_iota(jnp.int32, (R, HQ * HEAD_DIM), 1) // HEAD_DIM
    own = row_q == lane_q
    t_row = lax.broadcasted_iota(jnp.int32, (R, l), 0) % l
    j_col = lax.broadcasted_iota(jnp.int32, (R, l), 1)
    q = q_ref[0]

    def cached(ref, qd):
        heads = [ref[0, pl.ds(qd * HQ + j, P, stride=N_HEADS), :] for j in range(HQ)]
        return jnp.concatenate(heads, axis=1).astype(BF16)

    outs = []
    for qd in range(N_HEADS // HQ):
        cols = slice(qd * HQ * HEAD_DIM, (qd + 1) * HQ * HEAD_DIM)
        sel = lane_head == row_head + qd * HQ
        qx = jnp.concatenate([q[:, cols]] * HQ, axis=0)
        qx = jnp.where(own, qx, jnp.zeros_like(qx))
        cn_col = jnp.sum(jnp.where(sel, cn_rows, 0.0), axis=1, keepdims=True)
        cn_keys = _sel_right_nt(sel.astype(BF16), cn)
        past_bias = jnp.concatenate(
            [jnp.broadcast_to(suffix[qd * HQ + j:qd * HQ + j + 1, :], (l, P)) for j in range(HQ)], axis=0)
        s_p = _dot_nt(qx, cached(ck_ref, qd)) + past_bias + cn_col
        s_n = _dot_nt(qx, kn_ref[0, :, cols].astype(BF16)) + cn_col - cn_keys
        s_n = jnp.where(j_col <= t_row, s_n, -jnp.inf)
        m = jnp.maximum(jnp.max(s_p, axis=1, keepdims=True), jnp.max(s_n, axis=1, keepdims=True))
        p_p = jnp.exp(s_p - m)
        p_n = jnp.exp(s_n - m)
        den = jnp.sum(p_p, axis=1, keepdims=True) + jnp.sum(p_n, axis=1, keepdims=True)
        o = (_dot(p_p.astype(BF16), cached(cv_ref, qd))
             + _dot(p_n.astype(BF16), vn_ref[0, :, cols].astype(BF16))) * (1.0 / den)
        o = jnp.where(own, o, 0.0)
        acc = o[0:l]
        for j in range(1, HQ):
            acc = acc + o[j * l:(j + 1) * l]
        outs.append(acc)
    o_ref[0] = jnp.concatenate(outs, axis=1).astype(BF16)


def _attn_sample(q, kn, vn, lfn, cache_k, cache_v, cache_lfT):
    B, l, _ = q.shape
    P = cache_k.shape[1] // N_HEADS
    seq = lambda a, b: pl.BlockSpec((1, a, b), lambda i: (i, 0, 0))
    return pl.pallas_call(
        functools.partial(_attn_sample_kernel, l=l, P=P),
        grid=(B,),
        in_specs=[seq(l, D_ATTN), seq(l, D_ATTN), seq(l, D_ATTN), seq(l, N_HEADS),
                  seq(P * N_HEADS, HEAD_DIM), seq(P * N_HEADS, HEAD_DIM), seq(N_HEADS, P)],
        out_specs=seq(l, D_ATTN),
        out_shape=jax.ShapeDtypeStruct((B, l, D_ATTN), BF16),
        compiler_params=_params(1),
        name="fox_attn_sample",
    )(q, kn, vn, lfn, cache_k, cache_v, cache_lfT)


def _row(v):
    return v.reshape(1, -1).astype(F32)


def _col(v):
    return v.reshape(-1, 1).astype(F32)


def _prep_weights(norm_mix_w, norm_ffn_w, mamba_w_in, mamba_conv_w, mamba_conv_b, mamba_dt_bias,
                  mamba_A_log, mamba_D, mamba_norm_w, mamba_w_out, kv_norm_w, w_k, w_v, w_f, b_f,
                  attn_w_q, attn_w_o, ffn_w_up, ffn_conv_w, ffn_conv_b, ffn_w_down, final_norm_w):
    w_in = mamba_w_in[0]
    mamba = {
        "norm": _row(norm_mix_w[0]),
        "w_in": jnp.pad(w_in, ((0, 0), (0, IN_PROJ_PAD - IN_PROJ_DIM))).astype(BF16),
        "w_dtT": w_in[:, D_INNER + CONV_DIM:].T.astype(BF16),
        "conv_w": mamba_conv_w[0], "conv_b": _row(mamba_conv_b[0]),
        "dt_b": _row(mamba_dt_bias[0]), "dt_bT": _col(mamba_dt_bias[0]),
        "a_log": _row(mamba_A_log[0]), "a_logT": _col(mamba_A_log[0]),
        "d_x": _row(jnp.repeat(mamba_D[0], M_HEAD_DIM)),
        "gn_w": _row(mamba_norm_w[0]),
        "w_out": mamba_w_out[0].astype(BF16),
        "ex": jnp.repeat(jnp.eye(M_HEADS, dtype=BF16), M_HEAD_DIM, axis=1),
    }
    ffn = [{"norm": _row(norm_ffn_w[i]), "w_up": ffn_w_up[i].astype(BF16), "conv_w": ffn_conv_w[i],
            "conv_b": _row(ffn_conv_b[i]), "w_down": ffn_w_down[i].astype(BF16)} for i in range(2)]

    def place(offset, value):
        m = jnp.zeros((LANE // N_HEADS, N_HEADS, N_HEADS, AUG), F32)
        for i in range(3):
            m = m.at[i, :, :, offset + i].set(value * jnp.eye(N_HEADS, dtype=F32))
        return m.reshape(LANE, N_HEADS * AUG).astype(BF16)

    def three_groups(a):
        return jnp.pad(jnp.concatenate([a] * 3, axis=-1), ((0, 0), (0, LANE - 3 * N_HEADS)))

    def ones_at(offset):
        v = jnp.zeros((N_HEADS, AUG), F32).at[:, offset:offset + 3].set(1.0)
        return v.reshape(1, N_HEADS * AUG)

    w_q = attn_w_q[0] * ATTN_SCALE
    proj = {
        "kv_norm": _row(kv_norm_w), "mix_norm": _row(norm_mix_w[1]),
        "w_k": w_k.astype(BF16), "w_v": w_v.astype(BF16),
        "w_f": jnp.pad(w_f, ((0, 0), (0, LANE - N_HEADS))).astype(BF16), "b_f": _row(b_f),
        "w_f3": three_groups(w_f).astype(BF16), "b_f3": three_groups(_row(b_f)),
        "w_q": w_q.astype(BF16), "w_q_log2": (w_q * LOG2E).astype(BF16),
        "p_k": place(HEAD_DIM, -1.0), "p_q": place(HEAD_DIM + 3, 1.0),
        "one_k": ones_at(HEAD_DIM + 3), "one_q": ones_at(HEAD_DIM),
    }
    return mamba, ffn, proj, attn_w_o[0].astype(BF16), _row(final_norm_w)


PROMPT_TILE = 256
PROMPT_CHUNK = 128
ATTN_TILE = 256
ATTN_HEADS = 4
ATTN_UNROLL = 4
BF16_SUBLANES = 16
SAMPLE_SEQS_MAMBA = 4
SAMPLE_SEQS_FFN = 16


def _trunk_prompt(x, weights):
    mamba, ffn, proj, w_o, final_w = weights
    B, Ltot, _ = x.shape
    T = min(PROMPT_TILE, Ltot)
    conv0 = jnp.zeros((B, M_CONV_W - 1, CONV_DIM), F32)
    ssm0 = jnp.zeros((B, M_HEADS, M_HEAD_DIM, D_STATE), F32)
    fconv0 = jnp.zeros((B, F_CONV_W - 1, 2 * D_FF), F32)
    h, mconv, ssm = _mamba_layer(x, conv0, ssm0, mamba, nb=1, L=T, Lc=min(PROMPT_CHUNK, T))
    h, fconv_a = _ffn_layer(h, fconv0, ffn[0], nb=1, L=T)
    TA = min(ATTN_TILE, Ltot)
    k, v, logf, qa, ka, vT = _proj_prompt(h.reshape(B * Ltot, D_MODEL), proj, T=TA)
    o = _attn_prompt(qa, ka, vT, T=TA)
    y, fconv_b = _ffn_layer(h, fconv0, ffn[1], nb=1, L=T, o=o.reshape(B, Ltot, D_ATTN), w_o=w_o,
                            final_w=final_w)
    return (y, ssm[None], mconv[None], jnp.stack([fconv_a, fconv_b]),
            k.reshape(B, Ltot, N_HEADS, HEAD_DIM), v.reshape(B, Ltot, N_HEADS, HEAD_DIM),
            logf.reshape(B, Ltot, N_HEADS))


def _trunk_sample(x, state_ssm, state_mconv, state_fconv, cache_k, cache_v, cache_logf, weights):
    mamba, ffn, proj, w_o, final_w = weights
    B, l, _ = x.shape
    P = cache_k.shape[1]
    h, mconv, ssm = _mamba_layer(x, state_mconv[0], state_ssm[0], mamba,
                                 nb=min(SAMPLE_SEQS_MAMBA, B), L=l, Lc=l)
    nbf = min(SAMPLE_SEQS_FFN, B)
    h, fconv_a = _ffn_layer(h, state_fconv[0], ffn[0], nb=nbf, L=l)
    k, v, logf, q = _proj_sample(h.reshape(B * l, D_MODEL), proj, T=nbf * l)
    o = _attn_sample(q.reshape(B, l, D_ATTN), k.reshape(B, l, D_ATTN), v.reshape(B, l, D_ATTN),
                     logf.reshape(B, l, N_HEADS), cache_k.reshape(B, P * N_HEADS, HEAD_DIM),
                     cache_v.reshape(B, P * N_HEADS, HEAD_DIM), jnp.swapaxes(cache_logf, 1, 2))
    y, fconv_b = _ffn_layer(h, state_fconv[1], ffn[1], nb=nbf, L=l, o=o, w_o=w_o, final_w=final_w)
    return (y, ssm[None], mconv[None], jnp.stack([fconv_a, fconv_b]),
            k.reshape(B, l, N_HEADS, HEAD_DIM), v.reshape(B, l, N_HEADS, HEAD_DIM),
            logf.reshape(B, l, N_HEADS))


def kernel(x_prompt, x_sample, state_ssm, state_mamba_conv, state_ffn_conv, cache_k, cache_v, cache_logf,
           norm_mix_w, norm_ffn_w, mamba_w_in, mamba_conv_w, mamba_conv_b, mamba_dt_bias, mamba_A_log,
           mamba_D, mamba_norm_w, mamba_w_out, kv_norm_w, w_k, w_v, w_f, b_f, attn_w_q, attn_w_o,
           ffn_w_up, ffn_conv_w, ffn_conv_b, ffn_w_down, final_norm_w):
    weights = _prep_weights(norm_mix_w, norm_ffn_w, mamba_w_in, mamba_conv_w, mamba_conv_b,
                            mamba_dt_bias, mamba_A_log, mamba_D, mamba_norm_w, mamba_w_out, kv_norm_w,
                            w_k, w_v, w_f, b_f, attn_w_q, attn_w_o, ffn_w_up, ffn_conv_w, ffn_conv_b,
                            ffn_w_down, final_norm_w)
    p = _trunk_prompt(x_prompt, weights)
    s = _trunk_sample(x_sample, state_ssm, state_mamba_conv, state_ffn_conv, cache_k, cache_v,
                      cache_logf, weights)
    return (p[0], s[0]) + p[1:] + s[1:]
```

```python
import functools

import jax
import jax.numpy as jnp
from jax import lax
from jax.experimental import pallas as pl
from jax.experimental.pallas import tpu as pltpu

F32 = jnp.float32
BF16 = jnp.bfloat16

EPS = 1e-6
D_MODEL = 1024
D_INNER = 2048
M_HEAD_DIM = 64
M_HEADS = 32
M_GROUPS = 4
HEADS_PER_GROUP = M_HEADS // M_GROUPS
D_STATE = 128
M_CONV_W = 4
CONV_DIM = D_INNER + 2 * M_GROUPS * D_STATE
IN_PROJ_DIM = D_INNER + CONV_DIM + M_HEADS
LANE = 128
IN_PROJ_PAD = -(-IN_PROJ_DIM // LANE) * LANE
HEAD_DIM = 64
N_HEADS = 16
D_ATTN = N_HEADS * HEAD_DIM
ATTN_SCALE = HEAD_DIM ** -0.5
LOG2E = 1.4426950408889634
D_FF = 2816
F_CONV_W = 3
HALO = 8
AUG = 2 * HEAD_DIM
VMEM_LIMIT = 56 * 1024 * 1024


def _dot(a, b):
    return jnp.dot(a, b, preferred_element_type=F32)


def _dot_nt(a, b):
    return lax.dot_general(a, b, (((1,), (1,)), ((), ())), preferred_element_type=F32)


def _dot_tn(a, b):
    return lax.dot_general(a, b, (((0,), (0,)), ((), ())), preferred_element_type=F32)


def _split3(x):
    hi = x.astype(BF16)
    r = x - hi.astype(F32)
    mid = r.astype(BF16)
    lo = (r - mid.astype(F32)).astype(BF16)
    return hi, mid, lo


def _sel_left(sel, x):
    hi, mid, lo = _split3(x)
    return _dot(sel, hi) + _dot(sel, mid) + _dot(sel, lo)


def _sel_right(x, sel):
    hi, mid, lo = _split3(x)
    return _dot(hi, sel) + _dot(mid, sel) + _dot(lo, sel)


def _sel_right_nt(sel, x):
    hi, mid, lo = _split3(x)
    return _dot_nt(sel, hi) + _dot_nt(sel, mid) + _dot_nt(sel, lo)


def _rmsnorm(x, w):
    return x * lax.rsqrt(jnp.mean(x * x, axis=-1, keepdims=True) + EPS) * w


def _silu(x):
    return x * (1.0 / (1.0 + jnp.exp(-x)))


def _softplus(x):
    return jnp.maximum(x, 0.0) + jnp.log1p(jnp.exp(-jnp.abs(x)))


def _tri(n, lower):
    r = lax.broadcasted_iota(jnp.int32, (n, n), 0)
    c = lax.broadcasted_iota(jnp.int32, (n, n), 1)
    return (r >= c) if lower else (r <= c)


def _const_spec(shape):
    nd = len(shape)
    return pl.BlockSpec(shape, lambda *_: (0,) * nd, pipeline_mode=pl.Buffered(1))


def _params(n_axes=2):
    return pltpu.CompilerParams(dimension_semantics=("arbitrary",) * n_axes,
                                vmem_limit_bytes=VMEM_LIMIT)


def _mamba_kernel(x_ref, conv0_ref, ssm0_ref, nw_ref, win_ref, wdtT_ref, cw_ref, cb_ref,
                  dtb_ref, dtbT_ref, alog_ref, alogT_ref, dx_ref, gnw_ref, wout_ref, ex_ref,
                  h_ref, convn_ref, ssmn_ref, fb_ref, y_ref, *, nb, L, Lc):
    M = nb * L
    lo = HALO - (M_CONV_W - 1)

    @pl.when(pl.program_id(1) == 0)
    def _():
        fb_ref[:, lo:HALO, :] = conv0_ref[...]
        ssmn_ref[...] = ssm0_ref[...]

    x = x_ref[...].reshape(M, D_MODEL)
    ub = _rmsnorm(x, nw_ref[...]).astype(BF16)
    zx = _dot(ub, win_ref[...])
    dtT = _softplus(_dot_nt(wdtT_ref[...], ub) + dtbT_ref[...])
    dt = _softplus(zx[:, D_INNER + CONV_DIM:IN_PROJ_DIM] + dtb_ref[...])

    fb_ref[:, HALO:HALO + L, :] = zx[:, D_INNER:D_INNER + CONV_DIM].reshape(nb, L, CONV_DIM)
    acc = cb_ref[...]
    for tap in range(M_CONV_W):
        acc = acc + fb_ref[:, lo + tap:lo + tap + L, :] * cw_ref[tap:tap + 1, :]
    halo = fb_ref[:, lo + L:HALO + L, :]
    fb_ref[:, lo:HALO, :] = halo
    convn_ref[...] = halo
    xbc = _silu(acc).reshape(M, CONV_DIM)
    xs = xbc[:, :D_INNER]
    bm = xbc[:, D_INNER:D_INNER + M_GROUPS * D_STATE].astype(BF16)
    cm = xbc[:, D_INNER + M_GROUPS * D_STATE:].astype(BF16)

    dA = dt * (-jnp.exp(alog_ref[...]))
    dAT = dtT * (-jnp.exp(alogT_ref[...]))
    tril = _tri(Lc, True)
    tril_b = tril.astype(BF16)
    triu_b = _tri(Lc, False).astype(BF16)
    ex = ex_ref[...]
    quad = lax.broadcasted_iota(jnp.int32, (Lc, 4 * M_HEAD_DIM), 1) // M_HEAD_DIM

    for s in range(nb):
        for c in range(L // Lc):
            r0 = s * L + c * Lc
            acum = _sel_left(tril_b, dA[r0:r0 + Lc])
            dtT_c = dtT[:, r0:r0 + Lc]
            acumT = _sel_right(dAT[:, r0:r0 + Lc], triu_b)
            acum_x = _sel_right(acum, ex)
            dt_x = _sel_right(dt[r0:r0 + Lc], ex)
            xs_c = xs[r0:r0 + Lc]
            xs_b = xs_c.astype(BF16)
            x_end = (xs_c * (jnp.exp(acum_x[Lc - 1:Lc, :] - acum_x) * dt_x)).astype(BF16)
            e_x = jnp.exp(acum_x)
            d_blk = jnp.exp(acumT[:, Lc - 1:Lc])
            for g in range(M_GROUPS):
                b_g = bm[r0:r0 + Lc, g * D_STATE:(g + 1) * D_STATE]
                c_g = cm[r0:r0 + Lc, g * D_STATE:(g + 1) * D_STATE]
                cb = _dot_nt(c_g, b_g)
                h0 = g * HEADS_PER_GROUP
                h_in = ssmn_ref[s, h0:h0 + HEADS_PER_GROUP].reshape(HEADS_PER_GROUP * M_HEAD_DIM, D_STATE)
                y_inter = _dot_nt(c_g, h_in.astype(BF16))
                for qd in range(HEADS_PER_GROUP // 4):
                    col = (h0 + 4 * qd) * M_HEAD_DIM
                    xq = xs_b[:, col:col + 4 * M_HEAD_DIM]
                    yq = jnp.zeros((Lc, 4 * M_HEAD_DIM), F32)
                    for j in range(4):
                        h = h0 + 4 * qd + j
                        seg = acum[:, h:h + 1] - acumT[h:h + 1, :]
                        w = cb * jnp.exp(jnp.where(tril, seg, -jnp.inf)) * dtT_c[h:h + 1, :]
                        yq = yq + _dot(w.astype(BF16), jnp.where(quad == j, xq, jnp.zeros_like(xq)))
                    y_ref[r0:r0 + Lc, col:col + 4 * M_HEAD_DIM] = (
                        yq + y_inter[:, qd * 4 * M_HEAD_DIM:(qd + 1) * 4 * M_HEAD_DIM]
                        * e_x[:, col:col + 4 * M_HEAD_DIM]
                        + dx_ref[:, col:col + 4 * M_HEAD_DIM] * xs_c[:, col:col + 4 * M_HEAD_DIM])
                st = _dot_tn(x_end[:, h0 * M_HEAD_DIM:(h0 + HEADS_PER_GROUP) * M_HEAD_DIM], b_g)
                for j in range(HEADS_PER_GROUP):
                    h = h0 + j
                    ssmn_ref[s, h] = (ssmn_ref[s, h] * d_blk[h:h + 1, :]
                                      + st[j * M_HEAD_DIM:(j + 1) * M_HEAD_DIM, :])

    yz = y_ref[...] * _silu(zx[:, :D_INNER])
    gs = D_INNER // M_GROUPS
    parts = []
    for g in range(M_GROUPS):
        blk = yz[:, g * gs:(g + 1) * gs]
        parts.append(_rmsnorm(blk, gnw_ref[:, g * gs:(g + 1) * gs]))
    yn = jnp.concatenate(parts, axis=1).astype(BF16)
    h_ref[...] = (x + _dot(yn, wout_ref[...])).reshape(nb, L, D_MODEL)


def _mamba_layer(x, conv0, ssm0, w, *, nb, L, Lc):
    B, Ltot, _ = x.shape
    grid = (B // nb, Ltot // L)
    consts = [w["norm"], w["w_in"], w["w_dtT"], w["conv_w"], w["conv_b"], w["dt_b"], w["dt_bT"],
              w["a_log"], w["a_logT"], w["d_x"], w["gn_w"], w["w_out"], w["ex"]]
    return pl.pallas_call(
        functools.partial(_mamba_kernel, nb=nb, L=L, Lc=Lc),
        grid=grid,
        in_specs=[pl.BlockSpec((nb, L, D_MODEL), lambda s, t: (s, t, 0)),
                  pl.BlockSpec((nb, M_CONV_W - 1, CONV_DIM), lambda s, t: (s, 0, 0)),
                  pl.BlockSpec((nb, M_HEADS, M_HEAD_DIM, D_STATE), lambda s, t: (s, 0, 0, 0))]
        + [_const_spec(c.shape) for c in consts],
        out_specs=[pl.BlockSpec((nb, L, D_MODEL), lambda s, t: (s, t, 0)),
                   pl.BlockSpec((nb, M_CONV_W - 1, CONV_DIM), lambda s, t: (s, 0, 0)),
                   pl.BlockSpec((nb, M_HEADS, M_HEAD_DIM, D_STATE), lambda s, t: (s, 0, 0, 0))],
        out_shape=[jax.ShapeDtypeStruct(x.shape, F32),
                   jax.ShapeDtypeStruct(conv0.shape, F32),
                   jax.ShapeDtypeStruct(ssm0.shape, F32)],
        scratch_shapes=[pltpu.VMEM((nb, HALO + L, CONV_DIM), F32),
                        pltpu.VMEM((nb * L, D_INNER), F32)],
        compiler_params=_params(),
        name="mamba_layer",
    )(x, conv0, ssm0, *consts)


def _ffn_kernel(*refs, nb, L, with_attn, final):
    it = iter(refs)
    h_ref = next(it)
    if with_attn:
        o_ref, wo_ref = next(it), next(it)
    fconv0_ref, nw_ref, wup_ref, cw_ref, cb_ref, wdown_ref = (next(it) for _ in range(6))
    if final:
        fnw_ref = next(it)
    out_ref, fconvn_ref, fb_ref = next(it), next(it), next(it)
    M = nb * L
    lo = HALO - (F_CONV_W - 1)

    @pl.when(pl.program_id(1) == 0)
    def _():
        fb_ref[:, lo:HALO, :] = fconv0_ref[...]

    h = h_ref[...].reshape(M, D_MODEL)
    if with_attn:
        h = h + _dot(o_ref[...].reshape(M, D_ATTN), wo_ref[...])
    ub = _rmsnorm(h, nw_ref[...]).astype(BF16)
    fb_ref[:, HALO:HALO + L, :] = _dot(ub, wup_ref[...]).reshape(nb, L, 2 * D_FF)
    acc = cb_ref[...]
    for tap in range(F_CONV_W):
        acc = acc + fb_ref[:, lo + tap:lo + tap + L, :] * cw_ref[tap:tap + 1, :]
    halo = fb_ref[:, lo + L:HALO + L, :]
    fb_ref[:, lo:HALO, :] = halo
    fconvn_ref[...] = halo
    a = acc.reshape(M, 2 * D_FF)
    act = (_silu(a[:, :D_FF]) * a[:, D_FF:]).astype(BF16)
    hn = h + _dot(act, wdown_ref[...])
    if final:
        hn = _rmsnorm(hn, fnw_ref[...])
    out_ref[...] = hn.reshape(nb, L, D_MODEL)


def _ffn_layer(h, fconv0, w, *, nb, L, o=None, w_o=None, final_w=None):
    B, Ltot, _ = h.shape
    grid = (B // nb, Ltot // L)
    with_attn, final = o is not None, final_w is not None
    tile = pl.BlockSpec((nb, L, D_MODEL), lambda s, t: (s, t, 0))
    args, specs = [h], [tile]
    if with_attn:
        args += [o, w_o]
        specs += [pl.BlockSpec((nb, L, D_ATTN), lambda s, t: (s, t, 0)), _const_spec(w_o.shape)]
    consts = [w["norm"], w["w_up"], w["conv_w"], w["conv_b"], w["w_down"]]
    args += [fconv0] + consts
    specs += [pl.BlockSpec((nb, F_CONV_W - 1, 2 * D_FF), lambda s, t: (s, 0, 0))]
    specs += [_const_spec(c.shape) for c in consts]
    if final:
        args.append(final_w)
        specs.append(_const_spec(final_w.shape))
    return pl.pallas_call(
        functools.partial(_ffn_kernel, nb=nb, L=L, with_attn=with_attn, final=final),
        grid=grid,
        in_specs=specs,
        out_specs=[tile, pl.BlockSpec((nb, F_CONV_W - 1, 2 * D_FF), lambda s, t: (s, 0, 0))],
        out_shape=[jax.ShapeDtypeStruct(h.shape, F32), jax.ShapeDtypeStruct(fconv0.shape, F32)],
        scratch_shapes=[pltpu.VMEM((nb, HALO + L, 2 * D_FF), F32)],
        compiler_params=_params(),
        name="conv_ffn",
    )(*args)


def _log_sigmoid(x):
    return -_softplus(-x)


def _proj_prompt_kernel(h_ref, kvw_ref, nmw_ref, wv_ref, wf_ref, bf_ref,
                        wk_ref, wq_ref, pk_ref, pq_ref, onek_ref, oneq_ref,
                        k_ref, v_ref, logf_ref, qa_ref, ka_ref, vT_ref, carry_ref, *, T):
    @pl.when(pl.program_id(0) == 0)
    def _():
        carry_ref[...] = jnp.zeros_like(carry_ref)

    h = h_ref[...]
    hk = _rmsnorm(h, kvw_ref[...]).astype(BF16)
    u1 = _rmsnorm(h, nmw_ref[...]).astype(BF16)
    v = _dot(hk, wv_ref[...])
    v_ref[...] = v
    vT_ref[0] = v.T.astype(BF16)
    k = _dot(hk, wk_ref[...])
    k_ref[...] = k
    low = lax.broadcasted_iota(jnp.int32, (T, AUG), 1) < HEAD_DIM

    def spread(a):
        slots = []
        for j in range(N_HEADS // 2):
            two = a[:, j * AUG:(j + 1) * AUG]
            slots += [jnp.where(low, two, 0.0), jnp.where(low, pltpu.roll(two, HEAD_DIM, axis=1), 0.0)]
        return jnp.concatenate(slots, axis=1)
    logf3 = _log_sigmoid(_dot(hk, wf_ref[...]) + bf_ref[...])
    logf_ref[...] = logf3[:, :N_HEADS]
    c3 = carry_ref[...] + _sel_left(_tri(T, True).astype(BF16), logf3)
    carry_ref[...] = c3[T - 1:T, :]
    c_hi, c_mid, c_lo = _split3(c3 * LOG2E)
    group = lax.broadcasted_iota(jnp.int32, (T, LANE), 1) // N_HEADS
    parts = jnp.where(group == 0, c_hi, jnp.where(group == 1, c_mid,
                      jnp.where(group == 2, c_lo, jnp.zeros_like(c_lo))))
    ka_ref[...] = (spread(k) + onek_ref[...] + _dot(parts, pk_ref[...])).astype(BF16)
    qa_ref[...] = (spread(_dot(u1, wq_ref[...])) + oneq_ref[...] + _dot(parts, pq_ref[...])).astype(BF16)


def _proj_prompt(h2d, w, *, T):
    Ltot = h2d.shape[0]
    consts = [w["kv_norm"], w["mix_norm"], w["w_v"], w["w_f3"], w["b_f3"],
              w["w_k"], w["w_q_log2"], w["p_k"], w["p_q"], w["one_k"], w["one_q"]]
    row = lambda n: pl.BlockSpec((T, n), lambda i: (i, 0))
    return pl.pallas_call(
        functools.partial(_proj_prompt_kernel, T=T),
        grid=(Ltot // T,),
        in_specs=[row(D_MODEL)] + [_const_spec(c.shape) for c in consts],
        out_specs=[row(D_ATTN), row(D_ATTN), row(N_HEADS), row(N_HEADS * AUG), row(N_HEADS * AUG),
                   pl.BlockSpec((1, D_ATTN, T), lambda i: (i, 0, 0))],
        out_shape=[jax.ShapeDtypeStruct((Ltot, D_ATTN), F32),
                   jax.ShapeDtypeStruct((Ltot, D_ATTN), F32),
                   jax.ShapeDtypeStruct((Ltot, N_HEADS), F32),
                   jax.ShapeDtypeStruct((Ltot, N_HEADS * AUG), BF16),
                   jax.ShapeDtypeStruct((Ltot, N_HEADS * AUG), BF16),
                   jax.ShapeDtypeStruct((Ltot // T, D_ATTN, T), BF16)],
        scratch_shapes=[pltpu.VMEM((1, LANE), F32)],
        compiler_params=_params(1),
        name="kvq_proj_prompt",
    )(h2d, *consts)


def _proj_sample_kernel(h_ref, kvw_ref, nmw_ref, wk_ref, wv_ref, wf_ref, bf_ref, wq_ref,
                        k_ref, v_ref, logf_ref, q_ref):
    h = h_ref[...]
    hk = _rmsnorm(h, kvw_ref[...]).astype(BF16)
    u1 = _rmsnorm(h, nmw_ref[...]).astype(BF16)
    k_ref[...] = _dot(hk, wk_ref[...])
    v_ref[...] = _dot(hk, wv_ref[...])
    logf_ref[...] = _log_sigmoid(_dot(hk, wf_ref[...])[:, :N_HEADS] + bf_ref[...])
    q_ref[...] = _dot(u1, wq_ref[...]).astype(BF16)


def _proj_sample(h2d, w, *, T):
    n = h2d.shape[0]
    consts = [w["kv_norm"], w["mix_norm"], w["w_k"], w["w_v"], w["w_f"], w["b_f"], w["w_q"]]
    row = lambda m: pl.BlockSpec((T, m), lambda i: (i, 0))
    return pl.pallas_call(
        _proj_sample_kernel,
        grid=(n // T,),
        in_specs=[row(D_MODEL)] + [_const_spec(c.shape) for c in consts],
        out_specs=[row(D_ATTN), row(D_ATTN), row(N_HEADS), row(D_ATTN)],
        out_shape=[jax.ShapeDtypeStruct((n, D_ATTN), F32), jax.ShapeDtypeStruct((n, D_ATTN), F32),
                   jax.ShapeDtypeStruct((n, N_HEADS), F32), jax.ShapeDtypeStruct((n, D_ATTN), BF16)],
        compiler_params=_params(1),
        name="kvq_proj_sample",
    )(h2d, *consts)


def _attn_prompt_kernel(qa_ref, ka_ref, vT_ref, o_ref, s_ref, acc_ref, oT_ref, *, T, nblk):
    qi = pl.program_id(1)
    heads = range(ATTN_HEADS)
    q = [qa_ref[:, hh * AUG:(hh + 1) * AUG] for hh in heads]
    key_minus_query = (lax.broadcasted_iota(jnp.int32, (T, T), 0)
                       - lax.broadcasted_iota(jnp.int32, (T, T), 1))
    ones = jnp.ones((BF16_SUBLANES, T), BF16)

    def produce(kb, slot):
        start = pl.multiple_of(jnp.minimum(kb, nblk - 1) * T, T)
        bmax = []
        for hh in heads:
            sT = _dot_nt(ka_ref[pl.ds(start, T), hh * AUG:(hh + 1) * AUG], q[hh])
            s_ref[slot, hh] = sT
            bmax.append(jnp.max(sT, axis=0, keepdims=True))
        return tuple(bmax)

    def consume(kb, slot, ms, bmax, masked):
        out = []
        for hh in heads:
            m = ms[hh]
            sT = s_ref[slot, hh]
            if masked:
                sT = jnp.where(key_minus_query <= (qi - kb) * T, sT, -jnp.inf)
                m_new = jnp.maximum(m, jnp.max(sT, axis=0, keepdims=True))
            else:
                m_new = jnp.maximum(m, bmax[hh])
            pT = jnp.exp2(sT - m_new).astype(BF16)
            vT = vT_ref[jnp.minimum(kb, nblk - 1), hh * HEAD_DIM:(hh + 1) * HEAD_DIM, :]
            acc_ref[hh] = jnp.exp2(m - m_new) * acc_ref[hh] + _dot(jnp.concatenate([vT, ones], axis=0), pT)
            out.append(m_new)
        return tuple(out)

    def step(kb, slot, state, masked=False):
        ms, bmax = state
        bmax_next = produce(kb + 1, 1 - slot)
        return consume(kb, slot, ms, bmax, masked), bmax_next

    def pair(i, state):
        return step(2 * i + 1, 1, step(2 * i, 0, state))

    def pairs(i, state):
        for j in range(ATTN_UNROLL):
            state = pair(ATTN_UNROLL * i + j, state)
        return state

    acc_ref[...] = jnp.zeros_like(acc_ref)
    init = tuple(jnp.full((1, T), -jnp.inf, F32) for _ in heads)
    n_pairs = qi // 2
    state = lax.fori_loop(0, n_pairs // ATTN_UNROLL, pairs, (init, produce(0, 0)))
    state = lax.fori_loop(n_pairs - n_pairs % ATTN_UNROLL, n_pairs, pair, state)
    last = 2 * n_pairs
    ms, _ = step(last, 0, state, True)
    consume(last + 1, 1, ms, None, True)
    for hh in heads:
        acc = acc_ref[hh]
        oT_ref[hh * HEAD_DIM:(hh + 1) * HEAD_DIM, :] = acc[:HEAD_DIM] * (1.0 / acc[HEAD_DIM:HEAD_DIM + 1])
    o_ref[...] = oT_ref[...].T.astype(BF16)


def _attn_prompt(qa, ka, vT, *, T):
    Ltot = qa.shape[0]
    nblk = Ltot // T
    hps = ATTN_HEADS
    return pl.pallas_call(
        functools.partial(_attn_prompt_kernel, T=T, nblk=nblk),
        grid=(N_HEADS // hps, nblk),
        in_specs=[pl.BlockSpec((T, hps * AUG), lambda p, i: (i, p)),
                  pl.BlockSpec((Ltot, hps * AUG), lambda p, i: (0, p), pipeline_mode=pl.Buffered(1)),
                  pl.BlockSpec((nblk, hps * HEAD_DIM, T), lambda p, i: (0, p, 0),
                               pipeline_mode=pl.Buffered(1))],
        out_specs=pl.BlockSpec((T, hps * HEAD_DIM), lambda p, i: (i, p)),
        out_shape=jax.ShapeDtypeStruct((Ltot, D_ATTN), BF16),
        scratch_shapes=[pltpu.VMEM((2, hps, T, T), F32),
                        pltpu.VMEM((hps, HEAD_DIM + BF16_SUBLANES, T), F32),
                        pltpu.VMEM((hps * HEAD_DIM, T), F32)],
        compiler_params=_params(),
        name="fox_attn_prompt",
    )(qa, ka, vT)


def _attn_sample_kernel(q_ref, kn_ref, vn_ref, lfn_ref, ckT_ref, cvT_ref, lfT_ref, o_ref, *, l, P):
    HQ = SAMPLE_HEADS
    R = HQ * l
    W = HQ * HEAD_DIM
    lane_p = lax.broadcasted_iota(jnp.int32, (N_HEADS, P), 1)
    suffix = lfT_ref[0]
    suffix = jnp.where(lane_p < P - 1, pltpu.roll(suffix, P - 1, axis=1), 0.0)
    d = 1
    while d < P:
        suffix = suffix + jnp.where(lane_p < P - d, pltpu.roll(suffix, P - d, axis=1), 0.0)
        d *= 2
    lfn = lfn_ref[0]
    cn = _sel_left(_tri(l, True).astype(BF16), lfn)
    cn_rows = jnp.concatenate([cn] * HQ, axis=0)
    row_head = lax.broadcasted_iota(jnp.int32, (R, N_HEADS), 0) // l
    lane_head = lax.broadcasted_iota(jnp.int32, (R, N_HEADS), 1)
    row_q = lax.broadcasted_iota(jnp.int32, (R, W), 0) // l
    lane_q = lax.broadcasted_iota(jnp.int32, (R, W), 1) // HEAD_DIM
    own = row_q == lane_q
    t_row = lax.broadcasted_iota(jnp.int32, (R, l), 0) % l
    j_col = lax.broadcasted_iota(jnp.int32, (R, l), 1)
    q = q_ref[0]

    outs = []
    for g in range(N_HEADS // HQ):
        cols = slice(g * W, (g + 1) * W)
        sel = lane_head == row_head + g * HQ
        qx = jnp.concatenate([q[:, cols]] * HQ, axis=0)
        qx = jnp.where(own, qx, jnp.zeros_like(qx))
        cn_col = jnp.sum(jnp.where(sel, cn_rows, 0.0), axis=1, keepdims=True)
        cn_keys = _sel_right_nt(sel.astype(BF16), cn)
        past_bias = jnp.concatenate(
            [jnp.broadcast_to(suffix[g * HQ + j:g * HQ + j + 1, :], (l, P)) for j in range(HQ)], axis=0)
        kT = ckT_ref[0, g * HQ:(g + 1) * HQ].reshape(W, P).astype(BF16)
        vT = cvT_ref[0, g * HQ:(g + 1) * HQ].reshape(W, P).astype(BF16)
        s_p = _dot(qx, kT) + past_bias + cn_col
        s_n = _dot_nt(qx, kn_ref[0, :, cols].astype(BF16)) + cn_col - cn_keys
        s_n = jnp.where(j_col <= t_row, s_n, -jnp.inf)
        m = jnp.maximum(jnp.max(s_p, axis=1, keepdims=True), jnp.max(s_n, axis=1, keepdims=True))
        p_p = jnp.exp(s_p - m)
        p_n = jnp.exp(s_n - m)
        den = jnp.sum(p_p, axis=1, keepdims=True) + jnp.sum(p_n, axis=1, keepdims=True)
        o_past = _dot_nt(vT, p_p.astype(BF16)).T
        o = (o_past + _dot(p_n.astype(BF16), vn_ref[0, :, cols].astype(BF16))) * (1.0 / den)
        o = jnp.where(own, o, 0.0)
        acc = o[0:l]
        for j in range(1, HQ):
            acc = acc + o[j * l:(j + 1) * l]
        outs.append(acc)
    o_ref[0] = jnp.concatenate(outs, axis=1).astype(BF16)


def _attn_sample(q, kn, vn, lfn, cache_kT, cache_vT, cache_lfT):
    B, l, _ = q.shape
    P = cache_kT.shape[3]
    seq = lambda a, b: pl.BlockSpec((1, a, b), lambda i: (i, 0, 0))
    cache = pl.BlockSpec((1, N_HEADS, HEAD_DIM, P), lambda i: (i, 0, 0, 0))
    return pl.pallas_call(
        functools.partial(_attn_sample_kernel, l=l, P=P),
        grid=(B,),
        in_specs=[seq(l, D_ATTN), seq(l, D_ATTN), seq(l, D_ATTN), seq(l, N_HEADS),
                  cache, cache, seq(N_HEADS, P)],
        out_specs=seq(l, D_ATTN),
        out_shape=jax.ShapeDtypeStruct((B, l, D_ATTN), BF16),
        compiler_params=_params(1),
        name="fox_attn_sample",
    )(q, kn, vn, lfn, cache_kT, cache_vT, cache_lfT)


def _row(v):
    return v.reshape(1, -1).astype(F32)


def _col(v):
    return v.reshape(-1, 1).astype(F32)


def _prep_weights(norm_mix_w, norm_ffn_w, mamba_w_in, mamba_conv_w, mamba_conv_b, mamba_dt_bias,
                  mamba_A_log, mamba_D, mamba_norm_w, mamba_w_out, kv_norm_w, w_k, w_v, w_f, b_f,
                  attn_w_q, attn_w_o, ffn_w_up, ffn_conv_w, ffn_conv_b, ffn_w_down, final_norm_w):
    w_in = mamba_w_in[0]
    mamba = {
        "norm": _row(norm_mix_w[0]),
        "w_in": jnp.pad(w_in, ((0, 0), (0, IN_PROJ_PAD - IN_PROJ_DIM))).astype(BF16),
        "w_dtT": w_in[:, D_INNER + CONV_DIM:].T.astype(BF16),
        "conv_w": mamba_conv_w[0], "conv_b": _row(mamba_conv_b[0]),
        "dt_b": _row(mamba_dt_bias[0]), "dt_bT": _col(mamba_dt_bias[0]),
        "a_log": _row(mamba_A_log[0]), "a_logT": _col(mamba_A_log[0]),
        "d_x": _row(jnp.repeat(mamba_D[0], M_HEAD_DIM)),
        "gn_w": _row(mamba_norm_w[0]),
        "w_out": mamba_w_out[0].astype(BF16),
        "ex": jnp.repeat(jnp.eye(M_HEADS, dtype=BF16), M_HEAD_DIM, axis=1),
    }
    ffn = [{"norm": _row(norm_ffn_w[i]), "w_up": ffn_w_up[i].astype(BF16), "conv_w": ffn_conv_w[i],
            "conv_b": _row(ffn_conv_b[i]), "w_down": ffn_w_down[i].astype(BF16)} for i in range(2)]

    def place(offset, value):
        m = jnp.zeros((LANE // N_HEADS, N_HEADS, N_HEADS, AUG), F32)
        for i in range(3):
            m = m.at[i, :, :, offset + i].set(value * jnp.eye(N_HEADS, dtype=F32))
        return m.reshape(LANE, N_HEADS * AUG).astype(BF16)

    def three_groups(a):
        return jnp.pad(jnp.concatenate([a] * 3, axis=-1), ((0, 0), (0, LANE - 3 * N_HEADS)))

    def ones_at(offset):
        v = jnp.zeros((N_HEADS, AUG), F32).at[:, offset:offset + 3].set(1.0)
        return v.reshape(1, N_HEADS * AUG)

    w_q = attn_w_q[0] * ATTN_SCALE
    proj = {
        "kv_norm": _row(kv_norm_w), "mix_norm": _row(norm_mix_w[1]),
        "w_k": w_k.astype(BF16), "w_v": w_v.astype(BF16),
        "w_f": jnp.pad(w_f, ((0, 0), (0, LANE - N_HEADS))).astype(BF16), "b_f": _row(b_f),
        "w_f3": three_groups(w_f).astype(BF16), "b_f3": three_groups(_row(b_f)),
        "w_q": w_q.astype(BF16), "w_q_log2": (w_q * LOG2E).astype(BF16),
        "p_k": place(HEAD_DIM, -1.0), "p_q": place(HEAD_DIM + 3, 1.0),
        "one_k": ones_at(HEAD_DIM + 3), "one_q": ones_at(HEAD_DIM),
    }
    return mamba, ffn, proj, attn_w_o[0].astype(BF16), _row(final_norm_w)


PROMPT_TILE = 256
PROMPT_CHUNK = 128
ATTN_TILE = 256
ATTN_HEADS = 4
ATTN_UNROLL = 4
BF16_SUBLANES = 16
SAMPLE_SEQS_MAMBA = 4
SAMPLE_SEQS_FFN = 16
SAMPLE_HEADS = 8


def _trunk_prompt(x, weights):
    mamba, ffn, proj, w_o, final_w = weights
    B, Ltot, _ = x.shape
    T = min(PROMPT_TILE, Ltot)
    conv0 = jnp.zeros((B, M_CONV_W - 1, CONV_DIM), F32)
    ssm0 = jnp.zeros((B, M_HEADS, M_HEAD_DIM, D_STATE), F32)
    fconv0 = jnp.zeros((B, F_CONV_W - 1, 2 * D_FF), F32)
    h, mconv, ssm = _mamba_layer(x, conv0, ssm0, mamba, nb=1, L=T, Lc=min(PROMPT_CHUNK, T))
    h, fconv_a = _ffn_layer(h, fconv0, ffn[0], nb=1, L=T)
    TA = min(ATTN_TILE, Ltot)
    k, v, logf, qa, ka, vT = _proj_prompt(h.reshape(B * Ltot, D_MODEL), proj, T=TA)
    o = _attn_prompt(qa, ka, vT, T=TA)
    y, fconv_b = _ffn_layer(h, fconv0, ffn[1], nb=1, L=T, o=o.reshape(B, Ltot, D_ATTN), w_o=w_o,
                            final_w=final_w)
    return (y, ssm[None], mconv[None], jnp.stack([fconv_a, fconv_b]),
            k.reshape(B, Ltot, N_HEADS, HEAD_DIM), v.reshape(B, Ltot, N_HEADS, HEAD_DIM),
            logf.reshape(B, Ltot, N_HEADS))


def _trunk_sample(x, state_ssm, state_mconv, state_fconv, cache_k, cache_v, cache_logf, weights):
    mamba, ffn, proj, w_o, final_w = weights
    B, l, _ = x.shape
    P = cache_k.shape[1]
    h, mconv, ssm = _mamba_layer(x, state_mconv[0], state_ssm[0], mamba,
                                 nb=min(SAMPLE_SEQS_MAMBA, B), L=l, Lc=l)
    nbf = min(SAMPLE_SEQS_FFN, B)
    h, fconv_a = _ffn_layer(h, state_fconv[0], ffn[0], nb=nbf, L=l)
    k, v, logf, q = _proj_sample(h.reshape(B * l, D_MODEL), proj, T=nbf * l)
    o = _attn_sample(q.reshape(B, l, D_ATTN), k.reshape(B, l, D_ATTN), v.reshape(B, l, D_ATTN),
                     logf.reshape(B, l, N_HEADS), jnp.transpose(cache_k, (0, 2, 3, 1)),
                     jnp.transpose(cache_v, (0, 2, 3, 1)), jnp.swapaxes(cache_logf, 1, 2))
    y, fconv_b = _ffn_layer(h, state_fconv[1], ffn[1], nb=nbf, L=l, o=o, w_o=w_o, final_w=final_w)
    return (y, ssm[None], mconv[None], jnp.stack([fconv_a, fconv_b]),
            k.reshape(B, l, N_HEADS, HEAD_DIM), v.reshape(B, l, N_HEADS, HEAD_DIM),
            logf.reshape(B, l, N_HEADS))


def kernel(x_prompt, x_sample, state_ssm, state_mamba_conv, state_ffn_conv, cache_k, cache_v, cache_logf,
           norm_mix_w, norm_ffn_w, mamba_w_in, mamba_conv_w, mamba_conv_b, mamba_dt_bias, mamba_A_log,
           mamba_D, mamba_norm_w, mamba_w_out, kv_norm_w, w_k, w_v, w_f, b_f, attn_w_q, attn_w_o,
           ffn_w_up, ffn_conv_w, ffn_conv_b, ffn_w_down, final_norm_w):
    weights = _prep_weights(norm_mix_w, norm_ffn_w, mamba_w_in, mamba_conv_w, mamba_conv_b,
                            mamba_dt_bias, mamba_A_log, mamba_D, mamba_norm_w, mamba_w_out, kv_norm_w,
                            w_k, w_v, w_f, b_f, attn_w_q, attn_w_o, ffn_w_up, ffn_conv_w, ffn_conv_b,
                            ffn_w_down, final_norm_w)
    p = _trunk_prompt(x_prompt, weights)
    s = _trunk_sample(x_sample, state_ssm, state_mamba_conv, state_ffn_conv, cache_k, cache_v,
                      cache_logf, weights)
    return (p[0], s[0]) + p[1:] + s[1:]
```

```python
import functools

import jax
import jax.numpy as jnp
from jax import lax
from jax.experimental import pallas as pl
from jax.experimental.pallas import tpu as pltpu

F32 = jnp.float32
BF16 = jnp.bfloat16

EPS = 1e-6
D_MODEL = 1024
D_INNER = 2048
M_HEAD_DIM = 64
M_HEADS = 32
M_GROUPS = 4
HEADS_PER_GROUP = M_HEADS // M_GROUPS
D_STATE = 128
M_CONV_W = 4
CONV_DIM = D_INNER + 2 * M_GROUPS * D_STATE
IN_PROJ_DIM = D_INNER + CONV_DIM + M_HEADS
LANE = 128
IN_PROJ_PAD = -(-IN_PROJ_DIM // LANE) * LANE
HEAD_DIM = 64
N_HEADS = 16
D_ATTN = N_HEADS * HEAD_DIM
ATTN_SCALE = HEAD_DIM ** -0.5
LOG2E = 1.4426950408889634
D_FF = 2816
F_CONV_W = 3
HALO = 8
AUG = 2 * HEAD_DIM
VMEM_LIMIT = 56 * 1024 * 1024


def _dot(a, b):
    return jnp.dot(a, b, preferred_element_type=F32)


def _dot_nt(a, b):
    return lax.dot_general(a, b, (((1,), (1,)), ((), ())), preferred_element_type=F32)


def _dot_tn(a, b):
    return lax.dot_general(a, b, (((0,), (0,)), ((), ())), preferred_element_type=F32)


def _split3(x):
    hi = x.astype(BF16)
    r = x - hi.astype(F32)
    mid = r.astype(BF16)
    lo = (r - mid.astype(F32)).astype(BF16)
    return hi, mid, lo


def _sel_left(sel, x):
    hi, mid, lo = _split3(x)
    return _dot(sel, hi) + _dot(sel, mid) + _dot(sel, lo)


def _sel_right(x, sel):
    hi, mid, lo = _split3(x)
    return _dot(hi, sel) + _dot(mid, sel) + _dot(lo, sel)


def _sel_right_nt(sel, x):
    hi, mid, lo = _split3(x)
    return _dot_nt(sel, hi) + _dot_nt(sel, mid) + _dot_nt(sel, lo)


def _rmsnorm(x, w):
    return x * lax.rsqrt(jnp.mean(x * x, axis=-1, keepdims=True) + EPS) * w


def _silu(x):
    return x * (1.0 / (1.0 + jnp.exp(-x)))


def _softplus(x):
    return jnp.maximum(x, 0.0) + jnp.log1p(jnp.exp(-jnp.abs(x)))


def _tri(n, lower):
    r = lax.broadcasted_iota(jnp.int32, (n, n), 0)
    c = lax.broadcasted_iota(jnp.int32, (n, n), 1)
    return (r >= c) if lower else (r <= c)


def _const_spec(shape):
    nd = len(shape)
    return pl.BlockSpec(shape, lambda *_: (0,) * nd, pipeline_mode=pl.Buffered(1))


def _params(n_axes=2):
    return pltpu.CompilerParams(dimension_semantics=("arbitrary",) * n_axes,
                                vmem_limit_bytes=VMEM_LIMIT)


def _mamba_kernel(x_ref, conv0_ref, ssm0_ref, nw_ref, win_ref, wdtT_ref, cw_ref, cb_ref,
                  dtb_ref, dtbT_ref, alog_ref, alogT_ref, dx_ref, gnw_ref, wout_ref, ex_ref,
                  h_ref, convn_ref, ssmn_ref, fb_ref, y_ref, *, nb, L, Lc):
    M = nb * L
    lo = HALO - (M_CONV_W - 1)

    @pl.when(pl.program_id(1) == 0)
    def _():
        fb_ref[:, lo:HALO, :] = conv0_ref[...]
        ssmn_ref[...] = ssm0_ref[...]

    x = x_ref[...].reshape(M, D_MODEL)
    ub = _rmsnorm(x, nw_ref[...]).astype(BF16)
    zx = _dot(ub, win_ref[...])
    dtT = _softplus(_dot_nt(wdtT_ref[...], ub) + dtbT_ref[...])
    dt = _softplus(zx[:, D_INNER + CONV_DIM:] + dtb_ref[...])

    fb_ref[:, HALO:HALO + L, :] = zx[:, D_INNER:D_INNER + CONV_DIM].reshape(nb, L, CONV_DIM)
    acc = cb_ref[...]
    for tap in range(M_CONV_W):
        acc = acc + fb_ref[:, lo + tap:lo + tap + L, :] * cw_ref[tap:tap + 1, :]
    halo = fb_ref[:, lo + L:HALO + L, :]
    fb_ref[:, lo:HALO, :] = halo
    convn_ref[...] = halo
    xbc = _silu(acc).reshape(M, CONV_DIM)
    xs = xbc[:, :D_INNER]
    bm = xbc[:, D_INNER:D_INNER + M_GROUPS * D_STATE].astype(BF16)
    cm = xbc[:, D_INNER + M_GROUPS * D_STATE:].astype(BF16)

    dA = dt * (-LOG2E * jnp.exp(alog_ref[...]))
    dAT = dtT * (-LOG2E * jnp.exp(alogT_ref[...]))
    tril = _tri(Lc, True)
    tril_b = tril.astype(BF16)
    triu_b = _tri(Lc, False).astype(BF16)
    ex = ex_ref[...]
    quad = lax.broadcasted_iota(jnp.int32, (Lc, 4 * M_HEAD_DIM), 1) // M_HEAD_DIM
    group = lax.broadcasted_iota(jnp.int32, (Lc, LANE), 1) // M_HEADS

    def expand(a3):
        hi, mid, lo = _split3(a3)
        parts = jnp.where(group == 0, hi, jnp.where(group == 1, mid,
                          jnp.where(group == 2, lo, jnp.zeros_like(lo))))
        return _dot(parts, ex)

    for s in range(nb):
        for c in range(L // Lc):
            r0 = s * L + c * Lc
            acum = _sel_left(tril_b, dA[r0:r0 + Lc])
            dtT_c = dtT[:, r0:r0 + Lc]
            acumT = _sel_right(dAT[:, r0:r0 + Lc], triu_b)
            acum_x = expand(acum)
            dt_x = expand(dt[r0:r0 + Lc])
            xs_c = xs[r0:r0 + Lc]
            x_end = (xs_c * (jnp.exp2(acum_x[Lc - 1:Lc, :] - acum_x) * dt_x)).astype(BF16)
            e_x = jnp.exp2(acum_x)
            d_blk = jnp.exp2(acumT[:, Lc - 1:Lc])
            for g in range(M_GROUPS):
                b_g = bm[r0:r0 + Lc, g * D_STATE:(g + 1) * D_STATE]
                c_g = cm[r0:r0 + Lc, g * D_STATE:(g + 1) * D_STATE]
                cb = _dot_nt(c_g, b_g)
                h0 = g * HEADS_PER_GROUP
                h_in = ssmn_ref[s, h0:h0 + HEADS_PER_GROUP].reshape(HEADS_PER_GROUP * M_HEAD_DIM, D_STATE)
                y_inter = _dot_nt(c_g, h_in.astype(BF16))
                for qd in range(HEADS_PER_GROUP // 4):
                    col = (h0 + 4 * qd) * M_HEAD_DIM
                    xq = xs_c[:, col:col + 4 * M_HEAD_DIM]
                    yq = jnp.zeros((Lc, 4 * M_HEAD_DIM), F32)
                    for j in range(4):
                        h = h0 + 4 * qd + j
                        seg = acum[:, h:h + 1] - acumT[h:h + 1, :]
                        w = cb * jnp.exp2(jnp.where(tril, seg, -jnp.inf)) * dtT_c[h:h + 1, :]
                        yq = yq + _dot(w.astype(BF16), jnp.where(quad == j, xq, 0.0).astype(BF16))
                    y_ref[r0:r0 + Lc, col:col + 4 * M_HEAD_DIM] = (
                        yq + y_inter[:, qd * 4 * M_HEAD_DIM:(qd + 1) * 4 * M_HEAD_DIM]
                        * e_x[:, col:col + 4 * M_HEAD_DIM]
                        + dx_ref[:, col:col + 4 * M_HEAD_DIM] * xs_c[:, col:col + 4 * M_HEAD_DIM])
                st = _dot_tn(x_end[:, h0 * M_HEAD_DIM:(h0 + HEADS_PER_GROUP) * M_HEAD_DIM], b_g)
                for j in range(HEADS_PER_GROUP):
                    h = h0 + j
                    ssmn_ref[s, h] = (ssmn_ref[s, h] * d_blk[h:h + 1, :]
                                      + st[j * M_HEAD_DIM:(j + 1) * M_HEAD_DIM, :])

    yz = y_ref[...] * _silu(zx[:, :D_INNER])
    gs = D_INNER // M_GROUPS
    parts = []
    for g in range(M_GROUPS):
        blk = yz[:, g * gs:(g + 1) * gs]
        parts.append(_rmsnorm(blk, gnw_ref[:, g * gs:(g + 1) * gs]))
    yn = jnp.concatenate(parts, axis=1).astype(BF16)
    h_ref[...] = (x + _dot(yn, wout_ref[...])).reshape(nb, L, D_MODEL)


def _mamba_layer(x, conv0, ssm0, w, *, nb, L, Lc):
    B, Ltot, _ = x.shape
    grid = (B // nb, Ltot // L)
    consts = [w["norm"], w["w_in"], w["w_dtT"], w["conv_w"], w["conv_b"], w["dt_b"], w["dt_bT"],
              w["a_log"], w["a_logT"], w["d_x"], w["gn_w"], w["w_out"], w["ex"]]
    return pl.pallas_call(
        functools.partial(_mamba_kernel, nb=nb, L=L, Lc=Lc),
        grid=grid,
        in_specs=[pl.BlockSpec((nb, L, D_MODEL), lambda s, t: (s, t, 0)),
                  pl.BlockSpec((nb, M_CONV_W - 1, CONV_DIM), lambda s, t: (s, 0, 0)),
                  pl.BlockSpec((nb, M_HEADS, M_HEAD_DIM, D_STATE), lambda s, t: (s, 0, 0, 0))]
        + [_const_spec(c.shape) for c in consts],
        out_specs=[pl.BlockSpec((nb, L, D_MODEL), lambda s, t: (s, t, 0)),
                   pl.BlockSpec((nb, M_CONV_W - 1, CONV_DIM), lambda s, t: (s, 0, 0)),
                   pl.BlockSpec((nb, M_HEADS, M_HEAD_DIM, D_STATE), lambda s, t: (s, 0, 0, 0))],
        out_shape=[jax.ShapeDtypeStruct(x.shape, F32),
                   jax.ShapeDtypeStruct(conv0.shape, F32),
                   jax.ShapeDtypeStruct(ssm0.shape, F32)],
        scratch_shapes=[pltpu.VMEM((nb, HALO + L, CONV_DIM), F32),
                        pltpu.VMEM((nb * L, D_INNER), F32)],
        compiler_params=_params(),
        name="mamba_layer",
    )(x, conv0, ssm0, *consts)


def _ffn_kernel(*refs, nb, L, with_attn, final):
    it = iter(refs)
    h_ref = next(it)
    if with_attn:
        o_ref, wo_ref = next(it), next(it)
    fconv0_ref, nw_ref, wup_ref, cw_ref, cb_ref, wdown_ref = (next(it) for _ in range(6))
    if final:
        fnw_ref = next(it)
    out_ref, fconvn_ref, fb_ref = next(it), next(it), next(it)
    M = nb * L
    lo = HALO - (F_CONV_W - 1)

    @pl.when(pl.program_id(1) == 0)
    def _():
        fb_ref[:, lo:HALO, :] = fconv0_ref[...]

    h = h_ref[...].reshape(M, D_MODEL)
    if with_attn:
        h = h + _dot(o_ref[...].reshape(M, D_ATTN), wo_ref[...])
    ub = _rmsnorm(h, nw_ref[...]).astype(BF16)
    fb_ref[:, HALO:HALO + L, :] = _dot(ub, wup_ref[...]).reshape(nb, L, 2 * D_FF)
    acc = cb_ref[...]
    for tap in range(F_CONV_W):
        acc = acc + fb_ref[:, lo + tap:lo + tap + L, :] * cw_ref[tap:tap + 1, :]
    halo = fb_ref[:, lo + L:HALO + L, :]
    fb_ref[:, lo:HALO, :] = halo
    fconvn_ref[...] = halo
    a = acc.reshape(M, 2 * D_FF)
    act = (_silu(a[:, :D_FF]) * a[:, D_FF:]).astype(BF16)
    hn = h + _dot(act, wdown_ref[...])
    if final:
        hn = _rmsnorm(hn, fnw_ref[...])
    out_ref[...] = hn.reshape(nb, L, D_MODEL)


def _ffn_layer(h, fconv0, w, *, nb, L, o=None, w_o=None, final_w=None):
    B, Ltot, _ = h.shape
    grid = (B // nb, Ltot // L)
    with_attn, final = o is not None, final_w is not None
    tile = pl.BlockSpec((nb, L, D_MODEL), lambda s, t: (s, t, 0))
    args, specs = [h], [tile]
    if with_attn:
        args += [o, w_o]
        specs += [pl.BlockSpec((nb, L, D_ATTN), lambda s, t: (s, t, 0)), _const_spec(w_o.shape)]
    consts = [w["norm"], w["w_up"], w["conv_w"], w["conv_b"], w["w_down"]]
    args += [fconv0] + consts
    specs += [pl.BlockSpec((nb, F_CONV_W - 1, 2 * D_FF), lambda s, t: (s, 0, 0))]
    specs += [_const_spec(c.shape) for c in consts]
    if final:
        args.append(final_w)
        specs.append(_const_spec(final_w.shape))
    return pl.pallas_call(
        functools.partial(_ffn_kernel, nb=nb, L=L, with_attn=with_attn, final=final),
        grid=grid,
        in_specs=specs,
        out_specs=[tile, pl.BlockSpec((nb, F_CONV_W - 1, 2 * D_FF), lambda s, t: (s, 0, 0))],
        out_shape=[jax.ShapeDtypeStruct(h.shape, F32), jax.ShapeDtypeStruct(fconv0.shape, F32)],
        scratch_shapes=[pltpu.VMEM((nb, HALO + L, 2 * D_FF), F32)],
        compiler_params=_params(),
        name="conv_ffn",
    )(*args)


def _log_sigmoid(x):
    return -_softplus(-x)


def _proj_prompt_kernel(h_ref, kvw_ref, nmw_ref, wv_ref, wf_ref, bf_ref,
                        wk_ref, wq_ref, pk_ref, pq_ref, onek_ref, oneq_ref,
                        k_ref, v_ref, logf_ref, qa_ref, ka_ref, vT_ref, carry_ref, *, T):
    @pl.when(pl.program_id(0) == 0)
    def _():
        carry_ref[...] = jnp.zeros_like(carry_ref)

    h = h_ref[...]
    hk = _rmsnorm(h, kvw_ref[...]).astype(BF16)
    u1 = _rmsnorm(h, nmw_ref[...]).astype(BF16)
    v = _dot(hk, wv_ref[...])
    v_ref[...] = v
    vT_ref[0] = v.T.astype(BF16)
    k = _dot(hk, wk_ref[...])
    k_ref[...] = k
    low = lax.broadcasted_iota(jnp.int32, (T, AUG), 1) < HEAD_DIM

    def spread(a):
        slots = []
        for j in range(N_HEADS // 2):
            two = a[:, j * AUG:(j + 1) * AUG]
            slots += [jnp.where(low, two, 0.0), jnp.where(low, pltpu.roll(two, HEAD_DIM, axis=1), 0.0)]
        return jnp.concatenate(slots, axis=1)
    logf3 = _log_sigmoid(_dot(hk, wf_ref[...]) + bf_ref[...])
    logf_ref[...] = logf3[:, :N_HEADS]
    c3 = carry_ref[...] + _sel_left(_tri(T, True).astype(BF16), logf3)
    carry_ref[...] = c3[T - 1:T, :]
    c_hi, c_mid, c_lo = _split3(c3 * LOG2E)
    group = lax.broadcasted_iota(jnp.int32, (T, LANE), 1) // N_HEADS
    parts = jnp.where(group == 0, c_hi, jnp.where(group == 1, c_mid,
                      jnp.where(group == 2, c_lo, jnp.zeros_like(c_lo))))
    ka_ref[...] = (spread(k) + onek_ref[...] + _dot(parts, pk_ref[...])).astype(BF16)
    qa_ref[...] = (spread(_dot(u1, wq_ref[...])) + oneq_ref[...] + _dot(parts, pq_ref[...])).astype(BF16)


def _proj_prompt(h2d, w, *, T):
    Ltot = h2d.shape[0]
    consts = [w["kv_norm"], w["mix_norm"], w["w_v"], w["w_f3"], w["b_f3"],
              w["w_k"], w["w_q_log2"], w["p_k"], w["p_q"], w["one_k"], w["one_q"]]
    row = lambda n: pl.BlockSpec((T, n), lambda i: (i, 0))
    return pl.pallas_call(
        functools.partial(_proj_prompt_kernel, T=T),
        grid=(Ltot // T,),
        in_specs=[row(D_MODEL)] + [_const_spec(c.shape) for c in consts],
        out_specs=[row(D_ATTN), row(D_ATTN), row(N_HEADS), row(N_HEADS * AUG), row(N_HEADS * AUG),
                   pl.BlockSpec((1, D_ATTN, T), lambda i: (i, 0, 0))],
        out_shape=[jax.ShapeDtypeStruct((Ltot, D_ATTN), F32),
                   jax.ShapeDtypeStruct((Ltot, D_ATTN), F32),
                   jax.ShapeDtypeStruct((Ltot, N_HEADS), F32),
                   jax.ShapeDtypeStruct((Ltot, N_HEADS * AUG), BF16),
                   jax.ShapeDtypeStruct((Ltot, N_HEADS * AUG), BF16),
                   jax.ShapeDtypeStruct((Ltot // T, D_ATTN, T), BF16)],
        scratch_shapes=[pltpu.VMEM((1, LANE), F32)],
        compiler_params=_params(1),
        name="kvq_proj_prompt",
    )(h2d, *consts)


def _proj_sample_kernel(h_ref, kvw_ref, nmw_ref, wk_ref, wv_ref, wf_ref, bf_ref, wq_ref,
                        k_ref, v_ref, logf_ref, q_ref):
    h = h_ref[...]
    hk = _rmsnorm(h, kvw_ref[...]).astype(BF16)
    u1 = _rmsnorm(h, nmw_ref[...]).astype(BF16)
    k_ref[...] = _dot(hk, wk_ref[...])
    v_ref[...] = _dot(hk, wv_ref[...])
    logf_ref[...] = _log_sigmoid(_dot(hk, wf_ref[...])[:, :N_HEADS] + bf_ref[...])
    q_ref[...] = _dot(u1, wq_ref[...]).astype(BF16)


def _proj_sample(h2d, w, *, T):
    n = h2d.shape[0]
    consts = [w["kv_norm"], w["mix_norm"], w["w_k"], w["w_v"], w["w_f"], w["b_f"], w["w_q"]]
    row = lambda m: pl.BlockSpec((T, m), lambda i: (i, 0))
    return pl.pallas_call(
        _proj_sample_kernel,
        grid=(n // T,),
        in_specs=[row(D_MODEL)] + [_const_spec(c.shape) for c in consts],
        out_specs=[row(D_ATTN), row(D_ATTN), row(N_HEADS), row(D_ATTN)],
        out_shape=[jax.ShapeDtypeStruct((n, D_ATTN), F32), jax.ShapeDtypeStruct((n, D_ATTN), F32),
                   jax.ShapeDtypeStruct((n, N_HEADS), F32), jax.ShapeDtypeStruct((n, D_ATTN), BF16)],
        compiler_params=_params(1),
        name="kvq_proj_sample",
    )(h2d, *consts)


def _attn_prompt_kernel(qa_ref, ka_ref, vT_ref, o_ref, s_ref, acc_ref, oT_ref, *, T, nblk):
    qi = pl.program_id(1)
    heads = range(ATTN_HEADS)
    q = [qa_ref[:, hh * AUG:(hh + 1) * AUG] for hh in heads]
    key_minus_query = (lax.broadcasted_iota(jnp.int32, (T, T), 0)
                       - lax.broadcasted_iota(jnp.int32, (T, T), 1))
    ones = jnp.ones((BF16_SUBLANES, T), BF16)

    def produce(kb, slot):
        start = pl.multiple_of(jnp.minimum(kb, nblk - 1) * T, T)
        bmax = []
        for hh in heads:
            sT = _dot_nt(ka_ref[pl.ds(start, T), hh * AUG:(hh + 1) * AUG], q[hh])
            s_ref[slot, hh] = sT
            bmax.append(jnp.max(sT, axis=0, keepdims=True))
        return tuple(bmax)

    def consume(kb, slot, ms, bmax, masked):
        out = []
        for hh in heads:
            m = ms[hh]
            sT = s_ref[slot, hh]
            if masked:
                sT = jnp.where(key_minus_query <= (qi - kb) * T, sT, -jnp.inf)
                m_new = jnp.maximum(m, jnp.max(sT, axis=0, keepdims=True))
            else:
                m_new = jnp.maximum(m, bmax[hh])
            pT = jnp.exp2(sT - m_new).astype(BF16)
            vT = vT_ref[jnp.minimum(kb, nblk - 1), hh * HEAD_DIM:(hh + 1) * HEAD_DIM, :]
            acc_ref[hh] = jnp.exp2(m - m_new) * acc_ref[hh] + _dot(jnp.concatenate([vT, ones], axis=0), pT)
            out.append(m_new)
        return tuple(out)

    def step(kb, slot, state, masked=False):
        ms, bmax = state
        bmax_next = produce(kb + 1, 1 - slot)
        return consume(kb, slot, ms, bmax, masked), bmax_next

    def pair(i, state):
        return step(2 * i + 1, 1, step(2 * i, 0, state))

    def pairs(i, state):
        for j in range(ATTN_UNROLL):
            state = pair(ATTN_UNROLL * i + j, state)
        return state

    acc_ref[...] = jnp.zeros_like(acc_ref)
    init = tuple(jnp.full((1, T), -jnp.inf, F32) for _ in heads)
    n_pairs = qi // 2
    state = lax.fori_loop(0, n_pairs // ATTN_UNROLL, pairs, (init, produce(0, 0)))
    state = lax.fori_loop(n_pairs - n_pairs % ATTN_UNROLL, n_pairs, pair, state)
    last = 2 * n_pairs
    ms, _ = step(last, 0, state, True)

    @pl.when(qi % 2 == 1)
    def _():
        consume(last + 1, 1, ms, None, True)

    for hh in heads:
        acc = acc_ref[hh]
        oT_ref[hh * HEAD_DIM:(hh + 1) * HEAD_DIM, :] = acc[:HEAD_DIM] * (1.0 / acc[HEAD_DIM:HEAD_DIM + 1])
    o_ref[...] = oT_ref[...].T.astype(BF16)


def _attn_prompt(qa, ka, vT, *, T):
    Ltot = qa.shape[0]
    nblk = Ltot // T
    hps = ATTN_HEADS
    return pl.pallas_call(
        functools.partial(_attn_prompt_kernel, T=T, nblk=nblk),
        grid=(N_HEADS // hps, nblk),
        in_specs=[pl.BlockSpec((T, hps * AUG), lambda p, i: (i, p)),
                  pl.BlockSpec((Ltot, hps * AUG), lambda p, i: (0, p), pipeline_mode=pl.Buffered(1)),
                  pl.BlockSpec((nblk, hps * HEAD_DIM, T), lambda p, i: (0, p, 0),
                               pipeline_mode=pl.Buffered(1))],
        out_specs=pl.BlockSpec((T, hps * HEAD_DIM), lambda p, i: (i, p)),
        out_shape=jax.ShapeDtypeStruct((Ltot, D_ATTN), BF16),
        scratch_shapes=[pltpu.VMEM((2, hps, T, T), F32),
                        pltpu.VMEM((hps, HEAD_DIM + BF16_SUBLANES, T), F32),
                        pltpu.VMEM((hps * HEAD_DIM, T), F32)],
        compiler_params=_params(),
        name="fox_attn_prompt",
    )(qa, ka, vT)


def _attn_sample_kernel(q_ref, kn_ref, vn_ref, lfn_ref, ckT_ref, cvT_ref, lfT_ref, o_ref, *, l, P):
    HQ = SAMPLE_HEADS
    R = HQ * l
    W = HQ * HEAD_DIM
    lane_p = lax.broadcasted_iota(jnp.int32, (N_HEADS, P), 1)
    suffix = lfT_ref[0]
    suffix = jnp.where(lane_p < P - 1, pltpu.roll(suffix, P - 1, axis=1), 0.0)
    d = 1
    while d < P:
        suffix = suffix + jnp.where(lane_p < P - d, pltpu.roll(suffix, P - d, axis=1), 0.0)
        d *= 2
    lfn = lfn_ref[0]
    cn = _sel_left(_tri(l, True).astype(BF16), lfn)
    cn_rows = jnp.concatenate([cn] * HQ, axis=0)
    row_head = lax.broadcasted_iota(jnp.int32, (R, N_HEADS), 0) // l
    lane_head = lax.broadcasted_iota(jnp.int32, (R, N_HEADS), 1)
    row_q = lax.broadcasted_iota(jnp.int32, (R, W), 0) // l
    lane_q = lax.broadcasted_iota(jnp.int32, (R, W), 1) // HEAD_DIM
    own = row_q == lane_q
    t_row = lax.broadcasted_iota(jnp.int32, (R, l), 0) % l
    j_col = lax.broadcasted_iota(jnp.int32, (R, l), 1)
    q = q_ref[0]

    outs = []
    for g in range(N_HEADS // HQ):
        cols = slice(g * W, (g + 1) * W)
        sel = lane_head == row_head + g * HQ
        qx = jnp.concatenate([q[:, cols]] * HQ, axis=0)
        qx = jnp.where(own, qx, jnp.zeros_like(qx))
        cn_col = jnp.sum(jnp.where(sel, cn_rows, 0.0), axis=1, keepdims=True)
        cn_keys = _sel_right_nt(sel.astype(BF16), cn)
        past_bias = jnp.concatenate(
            [jnp.broadcast_to(suffix[g * HQ + j:g * HQ + j + 1, :], (l, P)) for j in range(HQ)], axis=0)
        kT = ckT_ref[0, g * HQ:(g + 1) * HQ].reshape(W, P).astype(BF16)
        vT = cvT_ref[0, g * HQ:(g + 1) * HQ].reshape(W, P).astype(BF16)
        s_p = _dot(qx, kT) + past_bias + cn_col
        s_n = _dot_nt(qx, kn_ref[0, :, cols].astype(BF16)) + cn_col - cn_keys
        s_n = jnp.where(j_col <= t_row, s_n, -jnp.inf)
        m = jnp.maximum(jnp.max(s_p, axis=1, keepdims=True), jnp.max(s_n, axis=1, keepdims=True))
        p_p = jnp.exp(s_p - m)
        p_n = jnp.exp(s_n - m)
        den = jnp.sum(p_p, axis=1, keepdims=True) + jnp.sum(p_n, axis=1, keepdims=True)
        o_past = _dot_nt(vT, p_p.astype(BF16)).T
        o = (o_past + _dot(p_n.astype(BF16), vn_ref[0, :, cols].astype(BF16))) * (1.0 / den)
        o = jnp.where(own, o, 0.0)
        acc = o[0:l]
        for j in range(1, HQ):
            acc = acc + o[j * l:(j + 1) * l]
        outs.append(acc)
    o_ref[0] = jnp.concatenate(outs, axis=1).astype(BF16)


def _attn_sample(q, kn, vn, lfn, cache_kT, cache_vT, cache_lfT):
    B, l, _ = q.shape
    P = cache_kT.shape[3]
    seq = lambda a, b: pl.BlockSpec((1, a, b), lambda i: (i, 0, 0))
    cache = pl.BlockSpec((1, N_HEADS, HEAD_DIM, P), lambda i: (i, 0, 0, 0))
    return pl.pallas_call(
        functools.partial(_attn_sample_kernel, l=l, P=P),
        grid=(B,),
        in_specs=[seq(l, D_ATTN), seq(l, D_ATTN), seq(l, D_ATTN), seq(l, N_HEADS),
                  cache, cache, seq(N_HEADS, P)],
        out_specs=seq(l, D_ATTN),
        out_shape=jax.ShapeDtypeStruct((B, l, D_ATTN), BF16),
        compiler_params=_params(1),
        name="fox_attn_sample",
    )(q, kn, vn, lfn, cache_kT, cache_vT, cache_lfT)


def _row(v):
    return v.reshape(1, -1).astype(F32)


def _col(v):
    return v.reshape(-1, 1).astype(F32)


def _prep_weights(norm_mix_w, norm_ffn_w, mamba_w_in, mamba_conv_w, mamba_conv_b, mamba_dt_bias,
                  mamba_A_log, mamba_D, mamba_norm_w, mamba_w_out, kv_norm_w, w_k, w_v, w_f, b_f,
                  attn_w_q, attn_w_o, ffn_w_up, ffn_conv_w, ffn_conv_b, ffn_w_down, final_norm_w):
    w_in = mamba_w_in[0]
    w_dt = w_in[:, D_INNER + CONV_DIM:]

    def head_groups(a):
        return jnp.pad(jnp.concatenate([a] * 3, axis=-1), ((0, 0), (0, LANE - 3 * M_HEADS)))

    ex = jnp.repeat(jnp.eye(M_HEADS, dtype=F32), M_HEAD_DIM, axis=1)
    mamba = {
        "norm": _row(norm_mix_w[0]),
        "w_in": jnp.concatenate([w_in[:, :D_INNER + CONV_DIM], head_groups(w_dt)], axis=1).astype(BF16),
        "w_dtT": w_dt.T.astype(BF16),
        "conv_w": mamba_conv_w[0], "conv_b": _row(mamba_conv_b[0]),
        "dt_b": head_groups(_row(mamba_dt_bias[0])), "dt_bT": _col(mamba_dt_bias[0]),
        "a_log": head_groups(_row(mamba_A_log[0])), "a_logT": _col(mamba_A_log[0]),
        "d_x": _row(jnp.repeat(mamba_D[0], M_HEAD_DIM)),
        "gn_w": _row(mamba_norm_w[0]),
        "w_out": mamba_w_out[0].astype(BF16),
        "ex": jnp.pad(jnp.concatenate([ex] * 3, axis=0), ((0, LANE - 3 * M_HEADS), (0, 0))).astype(BF16),
    }
    ffn = [{"norm": _row(norm_ffn_w[i]), "w_up": ffn_w_up[i].astype(BF16), "conv_w": ffn_conv_w[i],
            "conv_b": _row(ffn_conv_b[i]), "w_down": ffn_w_down[i].astype(BF16)} for i in range(2)]

    def place(offset, value):
        m = jnp.zeros((LANE // N_HEADS, N_HEADS, N_HEADS, AUG), F32)
        for i in range(3):
            m = m.at[i, :, :, offset + i].set(value * jnp.eye(N_HEADS, dtype=F32))
        return m.reshape(LANE, N_HEADS * AUG).astype(BF16)

    def three_groups(a):
        return jnp.pad(jnp.concatenate([a] * 3, axis=-1), ((0, 0), (0, LANE - 3 * N_HEADS)))

    def ones_at(offset):
        v = jnp.zeros((N_HEADS, AUG), F32).at[:, offset:offset + 3].set(1.0)
        return v.reshape(1, N_HEADS * AUG)

    w_q = attn_w_q[0] * ATTN_SCALE
    proj = {
        "kv_norm": _row(kv_norm_w), "mix_norm": _row(norm_mix_w[1]),
        "w_k": w_k.astype(BF16), "w_v": w_v.astype(BF16),
        "w_f": jnp.pad(w_f, ((0, 0), (0, LANE - N_HEADS))).astype(BF16), "b_f": _row(b_f),
        "w_f3": three_groups(w_f).astype(BF16), "b_f3": three_groups(_row(b_f)),
        "w_q": w_q.astype(BF16), "w_q_log2": (w_q * LOG2E).astype(BF16),
        "p_k": place(HEAD_DIM, -1.0), "p_q": place(HEAD_DIM + 3, 1.0),
        "one_k": ones_at(HEAD_DIM + 3), "one_q": ones_at(HEAD_DIM),
    }
    return mamba, ffn, proj, attn_w_o[0].astype(BF16), _row(final_norm_w)


MAMBA_TILE = 256
FFN_TILE = 512
PROMPT_CHUNK = 128
ATTN_TILE = 256
ATTN_HEADS = 4
ATTN_UNROLL = 4
BF16_SUBLANES = 16
SAMPLE_SEQS_MAMBA = 4
SAMPLE_SEQS_FFN = 16
SAMPLE_HEADS = 8


def _trunk_prompt(x, weights):
    mamba, ffn, proj, w_o, final_w = weights
    B, Ltot, _ = x.shape
    TM = min(MAMBA_TILE, Ltot)
    T = min(FFN_TILE, Ltot)
    conv0 = jnp.zeros((B, M_CONV_W - 1, CONV_DIM), F32)
    ssm0 = jnp.zeros((B, M_HEADS, M_HEAD_DIM, D_STATE), F32)
    fconv0 = jnp.zeros((B, F_CONV_W - 1, 2 * D_FF), F32)
    h, mconv, ssm = _mamba_layer(x, conv0, ssm0, mamba, nb=1, L=TM, Lc=min(PROMPT_CHUNK, TM))
    h, fconv_a = _ffn_layer(h, fconv0, ffn[0], nb=1, L=T)
    TA = min(ATTN_TILE, Ltot)
    k, v, logf, qa, ka, vT = _proj_prompt(h.reshape(B * Ltot, D_MODEL), proj, T=TA)
    o = _attn_prompt(qa, ka, vT, T=TA)
    y, fconv_b = _ffn_layer(h, fconv0, ffn[1], nb=1, L=T, o=o.reshape(B, Ltot, D_ATTN), w_o=w_o,
                            final_w=final_w)
    return (y, ssm[None], mconv[None], jnp.stack([fconv_a, fconv_b]),
            k.reshape(B, Ltot, N_HEADS, HEAD_DIM), v.reshape(B, Ltot, N_HEADS, HEAD_DIM),
            logf.reshape(B, Ltot, N_HEADS))


def _trunk_sample(x, state_ssm, state_mconv, state_fconv, cache_k, cache_v, cache_logf, weights):
    mamba, ffn, proj, w_o, final_w = weights
    B, l, _ = x.shape
    P = cache_k.shape[1]
    h, mconv, ssm = _mamba_layer(x, state_mconv[0], state_ssm[0], mamba,
                                 nb=min(SAMPLE_SEQS_MAMBA, B), L=l, Lc=l)
    nbf = min(SAMPLE_SEQS_FFN, B)
    h, fconv_a = _ffn_layer(h, state_fconv[0], ffn[0], nb=nbf, L=l)
    k, v, logf, q = _proj_sample(h.reshape(B * l, D_MODEL), proj, T=nbf * l)
    o = _attn_sample(q.reshape(B, l, D_ATTN), k.reshape(B, l, D_ATTN), v.reshape(B, l, D_ATTN),
                     logf.reshape(B, l, N_HEADS), jnp.transpose(cache_k, (0, 2, 3, 1)),
                     jnp.transpose(cache_v, (0, 2, 3, 1)), jnp.swapaxes(cache_logf, 1, 2))
    y, fconv_b = _ffn_layer(h, state_fconv[1], ffn[1], nb=nbf, L=l, o=o, w_o=w_o, final_w=final_w)
    return (y, ssm[None], mconv[None], jnp.stack([fconv_a, fconv_b]),
            k.reshape(B, l, N_HEADS, HEAD_DIM), v.reshape(B, l, N_HEADS, HEAD_DIM),
            logf.reshape(B, l, N_HEADS))


def kernel(x_prompt, x_sample, state_ssm, state_mamba_conv, state_ffn_conv, cache_k, cache_v, cache_logf,
           norm_mix_w, norm_ffn_w, mamba_w_in, mamba_conv_w, mamba_conv_b, mamba_dt_bias, mamba_A_log,
           mamba_D, mamba_norm_w, mamba_w_out, kv_norm_w, w_k, w_v, w_f, b_f, attn_w_q, attn_w_o,
           ffn_w_up, ffn_conv_w, ffn_conv_b, ffn_w_down, final_norm_w):
    weights = _prep_weights(norm_mix_w, norm_ffn_w, mamba_w_in, mamba_conv_w, mamba_conv_b,
                            mamba_dt_bias, mamba_A_log, mamba_D, mamba_norm_w, mamba_w_out, kv_norm_w,
                            w_k, w_v, w_f, b_f, attn_w_q, attn_w_o, ffn_w_up, ffn_conv_w, ffn_conv_b,
                            ffn_w_down, final_norm_w)
    p = _trunk_prompt(x_prompt, weights)
    s = _trunk_sample(x_sample, state_ssm, state_mamba_conv, state_ffn_conv, cache_k, cache_v,
                      cache_logf, weights)
    return (p[0], s[0]) + p[1:] + s[1:]
```

```python
import functools

import jax
import jax.numpy as jnp
from jax import lax
from jax.experimental import pallas as pl
from jax.experimental.pallas import tpu as pltpu

F32 = jnp.float32
BF16 = jnp.bfloat16

EPS = 1e-6
D_MODEL = 1024
D_INNER = 2048
M_HEAD_DIM = 64
M_HEADS = 32
M_GROUPS = 4
HEADS_PER_GROUP = M_HEADS // M_GROUPS
D_STATE = 128
M_CONV_W = 4
CONV_DIM = D_INNER + 2 * M_GROUPS * D_STATE
IN_PROJ_DIM = D_INNER + CONV_DIM + M_HEADS
LANE = 128
IN_PROJ_PAD = -(-IN_PROJ_DIM // LANE) * LANE
HEAD_DIM = 64
N_HEADS = 16
D_ATTN = N_HEADS * HEAD_DIM
ATTN_SCALE = HEAD_DIM ** -0.5
LOG2E = 1.4426950408889634
D_FF = 2816
F_CONV_W = 3
HALO = 8
AUG = 2 * HEAD_DIM
VMEM_LIMIT = 56 * 1024 * 1024


def _dot(a, b):
    return jnp.dot(a, b, preferred_element_type=F32)


def _dot_nt(a, b):
    return lax.dot_general(a, b, (((1,), (1,)), ((), ())), preferred_element_type=F32)


def _dot_tn(a, b):
    return lax.dot_general(a, b, (((0,), (0,)), ((), ())), preferred_element_type=F32)


def _split3(x):
    hi = x.astype(BF16)
    r = x - hi.astype(F32)
    mid = r.astype(BF16)
    lo = (r - mid.astype(F32)).astype(BF16)
    return hi, mid, lo


def _sel_left(sel, x):
    hi, mid, lo = _split3(x)
    return _dot(sel, hi) + _dot(sel, mid) + _dot(sel, lo)


def _sel_right(x, sel):
    hi, mid, lo = _split3(x)
    return _dot(hi, sel) + _dot(mid, sel) + _dot(lo, sel)


def _sel_right_nt(sel, x):
    hi, mid, lo = _split3(x)
    return _dot_nt(sel, hi) + _dot_nt(sel, mid) + _dot_nt(sel, lo)


def _rmsnorm(x, w):
    return x * lax.rsqrt(jnp.mean(x * x, axis=-1, keepdims=True) + EPS) * w


def _silu(x):
    return x * (1.0 / (1.0 + jnp.exp(-x)))


def _softplus(x):
    return jnp.maximum(x, 0.0) + jnp.log1p(jnp.exp(-jnp.abs(x)))


def _tri(n, lower):
    r = lax.broadcasted_iota(jnp.int32, (n, n), 0)
    c = lax.broadcasted_iota(jnp.int32, (n, n), 1)
    return (r >= c) if lower else (r <= c)


def _const_spec(shape):
    nd = len(shape)
    return pl.BlockSpec(shape, lambda *_: (0,) * nd, pipeline_mode=pl.Buffered(1))


def _params(n_axes=2):
    return pltpu.CompilerParams(dimension_semantics=("arbitrary",) * n_axes,
                                vmem_limit_bytes=VMEM_LIMIT)


def _mamba_kernel(x_ref, conv0_ref, ssm0_ref, nw_ref, win_ref, wdtT_ref, cw_ref, cb_ref,
                  dtb_ref, dtbT_ref, alog_ref, alogT_ref, dx_ref, gnw_ref, wout_ref, ex_ref,
                  h_ref, convn_ref, ssmn_ref, fb_ref, y_ref, *, nb, L, Lc):
    M = nb * L
    lo = HALO - (M_CONV_W - 1)

    @pl.when(pl.program_id(1) == 0)
    def _():
        fb_ref[:, lo:HALO, :] = conv0_ref[...]
        ssmn_ref[...] = ssm0_ref[...]

    x = x_ref[...].reshape(M, D_MODEL)
    ub = _rmsnorm(x, nw_ref[...]).astype(BF16)
    zx = _dot(ub, win_ref[...])
    dtT = _softplus(_dot_nt(wdtT_ref[...], ub) + dtbT_ref[...])
    dt = _softplus(zx[:, D_INNER + CONV_DIM:] + dtb_ref[...])

    fb_ref[:, HALO:HALO + L, :] = zx[:, D_INNER:D_INNER + CONV_DIM].reshape(nb, L, CONV_DIM)
    acc = cb_ref[...]
    for tap in range(M_CONV_W):
        acc = acc + fb_ref[:, lo + tap:lo + tap + L, :] * cw_ref[tap:tap + 1, :]
    halo = fb_ref[:, lo + L:HALO + L, :]
    fb_ref[:, lo:HALO, :] = halo
    convn_ref[...] = halo
    xbc = _silu(acc).reshape(M, CONV_DIM)
    xs = xbc[:, :D_INNER]
    bm = xbc[:, D_INNER:D_INNER + M_GROUPS * D_STATE].astype(BF16)
    cm = xbc[:, D_INNER + M_GROUPS * D_STATE:].astype(BF16)

    dA = dt * (-LOG2E * jnp.exp(alog_ref[...]))
    dAT = dtT * (-LOG2E * jnp.exp(alogT_ref[...]))
    tril = _tri(Lc, True)
    tril_b = tril.astype(BF16)
    triu_b = _tri(Lc, False).astype(BF16)
    ex = ex_ref[...]
    quad = lax.broadcasted_iota(jnp.int32, (Lc, 4 * M_HEAD_DIM), 1) // M_HEAD_DIM
    group = lax.broadcasted_iota(jnp.int32, (Lc, LANE), 1) // M_HEADS

    def expand(a3):
        hi, mid, lo = _split3(a3)
        parts = jnp.where(group == 0, hi, jnp.where(group == 1, mid,
                          jnp.where(group == 2, lo, jnp.zeros_like(lo))))
        return _dot(parts, ex)

    for s in range(nb):
        for c in range(L // Lc):
            r0 = s * L + c * Lc
            acum = _sel_left(tril_b, dA[r0:r0 + Lc])
            dtT_c = dtT[:, r0:r0 + Lc]
            acumT = _sel_right(dAT[:, r0:r0 + Lc], triu_b)
            acum_x = expand(acum)
            dt_x = expand(dt[r0:r0 + Lc])
            xs_c = xs[r0:r0 + Lc]
            x_end = (xs_c * (jnp.exp2(acum_x[Lc - 1:Lc, :] - acum_x) * dt_x)).astype(BF16)
            e_x = jnp.exp2(acum_x)
            d_blk = jnp.exp2(acumT[:, Lc - 1:Lc])
            for g in range(M_GROUPS):
                b_g = bm[r0:r0 + Lc, g * D_STATE:(g + 1) * D_STATE]
                c_g = cm[r0:r0 + Lc, g * D_STATE:(g + 1) * D_STATE]
                cb = _dot_nt(c_g, b_g)
                h0 = g * HEADS_PER_GROUP
                h_in = ssmn_ref[s, h0:h0 + HEADS_PER_GROUP].reshape(HEADS_PER_GROUP * M_HEAD_DIM, D_STATE)
                y_inter = _dot_nt(c_g, h_in.astype(BF16))
                for qd in range(HEADS_PER_GROUP // 4):
                    col = (h0 + 4 * qd) * M_HEAD_DIM
                    xq = xs_c[:, col:col + 4 * M_HEAD_DIM]
                    yq = jnp.zeros((Lc, 4 * M_HEAD_DIM), F32)
                    for j in range(4):
                        h = h0 + 4 * qd + j
                        seg = acum[:, h:h + 1] - acumT[h:h + 1, :]
                        w = cb * jnp.exp2(jnp.where(tril, seg, -jnp.inf)) * dtT_c[h:h + 1, :]
                        yq = yq + _dot(w.astype(BF16), jnp.where(quad == j, xq, 0.0).astype(BF16))
                    y_ref[r0:r0 + Lc, col:col + 4 * M_HEAD_DIM] = (
                        yq + y_inter[:, qd * 4 * M_HEAD_DIM:(qd + 1) * 4 * M_HEAD_DIM]
                        * e_x[:, col:col + 4 * M_HEAD_DIM]
                        + dx_ref[:, col:col + 4 * M_HEAD_DIM] * xs_c[:, col:col + 4 * M_HEAD_DIM])
                st = _dot_tn(x_end[:, h0 * M_HEAD_DIM:(h0 + HEADS_PER_GROUP) * M_HEAD_DIM], b_g)
                for j in range(HEADS_PER_GROUP):
                    h = h0 + j
                    ssmn_ref[s, h] = (ssmn_ref[s, h] * d_blk[h:h + 1, :]
                                      + st[j * M_HEAD_DIM:(j + 1) * M_HEAD_DIM, :])

    yz = y_ref[...] * _silu(zx[:, :D_INNER])
    gs = D_INNER // M_GROUPS
    parts = []
    for g in range(M_GROUPS):
        blk = yz[:, g * gs:(g + 1) * gs]
        parts.append(_rmsnorm(blk, gnw_ref[:, g * gs:(g + 1) * gs]))
    yn = jnp.concatenate(parts, axis=1).astype(BF16)
    h_ref[...] = (x + _dot(yn, wout_ref[...])).reshape(nb, L, D_MODEL)


def _mamba_layer(x, conv0, ssm0, w, *, nb, L, Lc):
    B, Ltot, _ = x.shape
    grid = (B // nb, Ltot // L)
    consts = [w["norm"], w["w_in"], w["w_dtT"], w["conv_w"], w["conv_b"], w["dt_b"], w["dt_bT"],
              w["a_log"], w["a_logT"], w["d_x"], w["gn_w"], w["w_out"], w["ex"]]
    return pl.pallas_call(
        functools.partial(_mamba_kernel, nb=nb, L=L, Lc=Lc),
        grid=grid,
        in_specs=[pl.BlockSpec((nb, L, D_MODEL), lambda s, t: (s, t, 0)),
                  pl.BlockSpec((nb, M_CONV_W - 1, CONV_DIM), lambda s, t: (s, 0, 0)),
                  pl.BlockSpec((nb, M_HEADS, M_HEAD_DIM, D_STATE), lambda s, t: (s, 0, 0, 0))]
        + [_const_spec(c.shape) for c in consts],
        out_specs=[pl.BlockSpec((nb, L, D_MODEL), lambda s, t: (s, t, 0)),
                   pl.BlockSpec((nb, M_CONV_W - 1, CONV_DIM), lambda s, t: (s, 0, 0)),
                   pl.BlockSpec((nb, M_HEADS, M_HEAD_DIM, D_STATE), lambda s, t: (s, 0, 0, 0))],
        out_shape=[jax.ShapeDtypeStruct(x.shape, F32),
                   jax.ShapeDtypeStruct(conv0.shape, F32),
                   jax.ShapeDtypeStruct(ssm0.shape, F32)],
        scratch_shapes=[pltpu.VMEM((nb, HALO + L, CONV_DIM), F32),
                        pltpu.VMEM((nb * L, D_INNER), F32)],
        compiler_params=_params(),
        name="mamba_layer",
    )(x, conv0, ssm0, *consts)


def _ffn_kernel(*refs, nb, L, with_attn, final):
    it = iter(refs)
    h_ref = next(it)
    if with_attn:
        o_ref, wo_ref = next(it), next(it)
    fconv0_ref, nw_ref, wup_ref, cw_ref, cb_ref, wdown_ref = (next(it) for _ in range(6))
    if final:
        fnw_ref = next(it)
    out_ref, fconvn_ref, fb_ref = next(it), next(it), next(it)
    M = nb * L
    lo = HALO - (F_CONV_W - 1)

    @pl.when(pl.program_id(1) == 0)
    def _():
        fb_ref[:, lo:HALO, :] = fconv0_ref[...]

    h = h_ref[...].reshape(M, D_MODEL)
    if with_attn:
        h = h + _dot(o_ref[...].reshape(M, D_ATTN), wo_ref[...])
    ub = _rmsnorm(h, nw_ref[...]).astype(BF16)
    fb_ref[:, HALO:HALO + L, :] = _dot(ub, wup_ref[...]).reshape(nb, L, 2 * D_FF)
    acc = cb_ref[...]
    for tap in range(F_CONV_W):
        acc = acc + fb_ref[:, lo + tap:lo + tap + L, :] * cw_ref[tap:tap + 1, :]
    halo = fb_ref[:, lo + L:HALO + L, :]
    fb_ref[:, lo:HALO, :] = halo
    fconvn_ref[...] = halo
    a = acc.reshape(M, 2 * D_FF)
    act = (_silu(a[:, :D_FF]) * a[:, D_FF:]).astype(BF16)
    hn = h + _dot(act, wdown_ref[...])
    if final:
        hn = _rmsnorm(hn, fnw_ref[...])
    out_ref[...] = hn.reshape(nb, L, D_MODEL)


def _ffn_layer(h, fconv0, w, *, nb, L, o=None, w_o=None, final_w=None):
    B, Ltot, _ = h.shape
    grid = (B // nb, Ltot // L)
    with_attn, final = o is not None, final_w is not None
    tile = pl.BlockSpec((nb, L, D_MODEL), lambda s, t: (s, t, 0))
    args, specs = [h], [tile]
    if with_attn:
        args += [o, w_o]
        specs += [pl.BlockSpec((nb, L, D_ATTN), lambda s, t: (s, t, 0)), _const_spec(w_o.shape)]
    consts = [w["norm"], w["w_up"], w["conv_w"], w["conv_b"], w["w_down"]]
    args += [fconv0] + consts
    specs += [pl.BlockSpec((nb, F_CONV_W - 1, 2 * D_FF), lambda s, t: (s, 0, 0))]
    specs += [_const_spec(c.shape) for c in consts]
    if final:
        args.append(final_w)
        specs.append(_const_spec(final_w.shape))
    return pl.pallas_call(
        functools.partial(_ffn_kernel, nb=nb, L=L, with_attn=with_attn, final=final),
        grid=grid,
        in_specs=specs,
        out_specs=[tile, pl.BlockSpec((nb, F_CONV_W - 1, 2 * D_FF), lambda s, t: (s, 0, 0))],
        out_shape=[jax.ShapeDtypeStruct(h.shape, F32), jax.ShapeDtypeStruct(fconv0.shape, F32)],
        scratch_shapes=[pltpu.VMEM((nb, HALO + L, 2 * D_FF), F32)],
        compiler_params=_params(),
        name="conv_ffn",
    )(*args)


def _log_sigmoid(x):
    return -_softplus(-x)


def _proj_prompt_kernel(h_ref, kvw_ref, nmw_ref, wv_ref, wf_ref, bf_ref,
                        wk_ref, wq_ref, pk_ref, pq_ref, onek_ref, oneq_ref,
                        k_ref, v_ref, logf_ref, qa_ref, ka_ref, vT_ref, carry_ref, *, T):
    @pl.when(pl.program_id(0) == 0)
    def _():
        carry_ref[...] = jnp.zeros_like(carry_ref)

    h = h_ref[...]
    hk = _rmsnorm(h, kvw_ref[...]).astype(BF16)
    u1 = _rmsnorm(h, nmw_ref[...]).astype(BF16)
    v = _dot(hk, wv_ref[...])
    v_ref[...] = v
    vT_ref[0] = v.T.astype(BF16)
    k = _dot(hk, wk_ref[...])
    k_ref[...] = k
    low = lax.broadcasted_iota(jnp.int32, (T, AUG), 1) < HEAD_DIM

    def spread(a):
        slots = []
        for j in range(N_HEADS // 2):
            two = a[:, j * AUG:(j + 1) * AUG]
            slots += [jnp.where(low, two, 0.0), jnp.where(low, pltpu.roll(two, HEAD_DIM, axis=1), 0.0)]
        return jnp.concatenate(slots, axis=1)
    logf3 = _log_sigmoid(_dot(hk, wf_ref[...]) + bf_ref[...])
    logf_ref[...] = logf3[:, :N_HEADS]
    c3 = carry_ref[...] + _sel_left(_tri(T, True).astype(BF16), logf3)
    carry_ref[...] = c3[T - 1:T, :]
    c_hi, c_mid, c_lo = _split3(c3 * LOG2E)
    group = lax.broadcasted_iota(jnp.int32, (T, LANE), 1) // N_HEADS
    parts = jnp.where(group == 0, c_hi, jnp.where(group == 1, c_mid,
                      jnp.where(group == 2, c_lo, jnp.zeros_like(c_lo))))
    ka_ref[...] = (spread(k) + onek_ref[...] + _dot(parts, pk_ref[...])).astype(BF16)
    qa_ref[...] = (spread(_dot(u1, wq_ref[...])) + oneq_ref[...] + _dot(parts, pq_ref[...])).astype(BF16)


def _proj_prompt(h2d, w, *, T):
    Ltot = h2d.shape[0]
    consts = [w["kv_norm"], w["mix_norm"], w["w_v"], w["w_f3"], w["b_f3"],
              w["w_k"], w["w_q_log2"], w["p_k"], w["p_q"], w["one_k"], w["one_q"]]
    row = lambda n: pl.BlockSpec((T, n), lambda i: (i, 0))
    return pl.pallas_call(
        functools.partial(_proj_prompt_kernel, T=T),
        grid=(Ltot // T,),
        in_specs=[row(D_MODEL)] + [_const_spec(c.shape) for c in consts],
        out_specs=[row(D_ATTN), row(D_ATTN), row(N_HEADS), row(N_HEADS * AUG), row(N_HEADS * AUG),
                   pl.BlockSpec((1, D_ATTN, T), lambda i: (i, 0, 0))],
        out_shape=[jax.ShapeDtypeStruct((Ltot, D_ATTN), F32),
                   jax.ShapeDtypeStruct((Ltot, D_ATTN), F32),
                   jax.ShapeDtypeStruct((Ltot, N_HEADS), F32),
                   jax.ShapeDtypeStruct((Ltot, N_HEADS * AUG), BF16),
                   jax.ShapeDtypeStruct((Ltot, N_HEADS * AUG), BF16),
                   jax.ShapeDtypeStruct((Ltot // T, D_ATTN, T), BF16)],
        scratch_shapes=[pltpu.VMEM((1, LANE), F32)],
        compiler_params=_params(1),
        name="kvq_proj_prompt",
    )(h2d, *consts)


def _proj_sample_kernel(h_ref, kvw_ref, nmw_ref, wk_ref, wv_ref, wf_ref, bf_ref, wq_ref,
                        k_ref, v_ref, logf_ref, q_ref):
    h = h_ref[...]
    hk = _rmsnorm(h, kvw_ref[...]).astype(BF16)
    u1 = _rmsnorm(h, nmw_ref[...]).astype(BF16)
    k_ref[...] = _dot(hk, wk_ref[...])
    v_ref[...] = _dot(hk, wv_ref[...])
    logf_ref[...] = _log_sigmoid(_dot(hk, wf_ref[...])[:, :N_HEADS] + bf_ref[...])
    q_ref[...] = _dot(u1, wq_ref[...]).astype(BF16)


def _proj_sample(h2d, w, *, T):
    n = h2d.shape[0]
    consts = [w["kv_norm"], w["mix_norm"], w["w_k"], w["w_v"], w["w_f"], w["b_f"], w["w_q"]]
    row = lambda m: pl.BlockSpec((T, m), lambda i: (i, 0))
    return pl.pallas_call(
        _proj_sample_kernel,
        grid=(n // T,),
        in_specs=[row(D_MODEL)] + [_const_spec(c.shape) for c in consts],
        out_specs=[row(D_ATTN), row(D_ATTN), row(N_HEADS), row(D_ATTN)],
        out_shape=[jax.ShapeDtypeStruct((n, D_ATTN), F32), jax.ShapeDtypeStruct((n, D_ATTN), F32),
                   jax.ShapeDtypeStruct((n, N_HEADS), F32), jax.ShapeDtypeStruct((n, D_ATTN), BF16)],
        compiler_params=_params(1),
        name="kvq_proj_sample",
    )(h2d, *consts)


def _attn_prompt_kernel(qa_ref, ka_ref, vT_ref, o_ref, s_ref, acc_ref, oT_ref, *, T, nblk):
    qi = pl.program_id(1)
    heads = range(ATTN_HEADS)
    q = [qa_ref[:, hh * AUG:(hh + 1) * AUG] for hh in heads]
    key_minus_query = (lax.broadcasted_iota(jnp.int32, (T, T), 0)
                       - lax.broadcasted_iota(jnp.int32, (T, T), 1))
    ones = jnp.ones((BF16_SUBLANES, T), BF16)

    def produce(kb, slot):
        start = pl.multiple_of(jnp.minimum(kb, nblk - 1) * T, T)
        bmax = []
        for hh in heads:
            sT = _dot_nt(ka_ref[pl.ds(start, T), hh * AUG:(hh + 1) * AUG], q[hh])
            s_ref[slot, hh] = sT
            bmax.append(jnp.max(sT, axis=0, keepdims=True))
        return tuple(bmax)

    def consume(kb, slot, ms, bmax, masked):
        out = []
        for hh in heads:
            m = ms[hh]
            sT = s_ref[slot, hh]
            if masked:
                sT = jnp.where(key_minus_query <= (qi - kb) * T, sT, -jnp.inf)
                m_new = jnp.maximum(m, jnp.max(sT, axis=0, keepdims=True))
            else:
                m_new = jnp.maximum(m, bmax[hh])
            pT = jnp.exp2(sT - m_new).astype(BF16)
            vT = vT_ref[jnp.minimum(kb, nblk - 1), hh * HEAD_DIM:(hh + 1) * HEAD_DIM, :]
            acc_ref[hh] = jnp.exp2(m - m_new) * acc_ref[hh] + _dot(jnp.concatenate([vT, ones], axis=0), pT)
            out.append(m_new)
        return tuple(out)

    def step(kb, slot, state, masked=False):
        ms, bmax = state
        bmax_next = produce(kb + 1, 1 - slot)
        return consume(kb, slot, ms, bmax, masked), bmax_next

    def pair(i, state):
        return step(2 * i + 1, 1, step(2 * i, 0, state))

    def pairs(i, state):
        for j in range(ATTN_UNROLL):
            state = pair(ATTN_UNROLL * i + j, state)
        return state

    acc_ref[...] = jnp.zeros_like(acc_ref)
    init = tuple(jnp.full((1, T), -jnp.inf, F32) for _ in heads)
    n_pairs = qi // 2
    state = lax.fori_loop(0, n_pairs // ATTN_UNROLL, pairs, (init, produce(0, 0)))
    rest = n_pairs % ATTN_UNROLL
    base = n_pairs - rest
    state = lax.fori_loop(0, rest // 2, lambda i, st: pair(base + 2 * i + 1, pair(base + 2 * i, st)), state)
    state = lax.fori_loop(n_pairs - rest % 2, n_pairs, pair, state)
    last = 2 * n_pairs
    ms, _ = step(last, 0, state, True)

    @pl.when(qi % 2 == 1)
    def _():
        consume(last + 1, 1, ms, None, True)

    for hh in heads:
        acc = acc_ref[hh]
        oT_ref[hh * HEAD_DIM:(hh + 1) * HEAD_DIM, :] = acc[:HEAD_DIM] * (1.0 / acc[HEAD_DIM:HEAD_DIM + 1])
    o_ref[...] = oT_ref[...].T.astype(BF16)


def _attn_prompt(qa, ka, vT, *, T):
    Ltot = qa.shape[0]
    nblk = Ltot // T
    hps = ATTN_HEADS
    return pl.pallas_call(
        functools.partial(_attn_prompt_kernel, T=T, nblk=nblk),
        grid=(N_HEADS // hps, nblk),
        in_specs=[pl.BlockSpec((T, hps * AUG), lambda p, i: (i, p)),
                  pl.BlockSpec((Ltot, hps * AUG), lambda p, i: (0, p)),
                  pl.BlockSpec((nblk, hps * HEAD_DIM, T), lambda p, i: (0, p, 0))],
        out_specs=pl.BlockSpec((T, hps * HEAD_DIM), lambda p, i: (i, p)),
        out_shape=jax.ShapeDtypeStruct((Ltot, D_ATTN), BF16),
        scratch_shapes=[pltpu.VMEM((2, hps, T, T), F32),
                        pltpu.VMEM((hps, HEAD_DIM + BF16_SUBLANES, T), F32),
                        pltpu.VMEM((hps * HEAD_DIM, T), F32)],
        compiler_params=_params(),
        name="fox_attn_prompt",
    )(qa, ka, vT)


def _attn_sample_kernel(q_ref, kn_ref, vn_ref, lfn_ref, ckT_ref, cvT_ref, lfT_ref, o_ref, *, l, P):
    HQ = SAMPLE_HEADS
    R = HQ * l
    W = HQ * HEAD_DIM
    lane_p = lax.broadcasted_iota(jnp.int32, (N_HEADS, P), 1)
    suffix = lfT_ref[0]
    suffix = jnp.where(lane_p < P - 1, pltpu.roll(suffix, P - 1, axis=1), 0.0)
    d = 1
    while d < P:
        suffix = suffix + jnp.where(lane_p < P - d, pltpu.roll(suffix, P - d, axis=1), 0.0)
        d *= 2
    lfn = lfn_ref[0]
    cn = _sel_left(_tri(l, True).astype(BF16), lfn)
    cn_rows = jnp.concatenate([cn] * HQ, axis=0)
    row_head = lax.broadcasted_iota(jnp.int32, (R, N_HEADS), 0) // l
    lane_head = lax.broadcasted_iota(jnp.int32, (R, N_HEADS), 1)
    row_q = lax.broadcasted_iota(jnp.int32, (R, W), 0) // l
    lane_q = lax.broadcasted_iota(jnp.int32, (R, W), 1) // HEAD_DIM
    own = row_q == lane_q
    t_row = lax.broadcasted_iota(jnp.int32, (R, l), 0) % l
    j_col = lax.broadcasted_iota(jnp.int32, (R, l), 1)
    q = q_ref[0]

    outs = []
    for g in range(N_HEADS // HQ):
        cols = slice(g * W, (g + 1) * W)
        sel = lane_head == row_head + g * HQ
        qx = jnp.concatenate([q[:, cols]] * HQ, axis=0)
        qx = jnp.where(own, qx, jnp.zeros_like(qx))
        cn_col = jnp.sum(jnp.where(sel, cn_rows, 0.0), axis=1, keepdims=True)
        cn_keys = _sel_right_nt(sel.astype(BF16), cn)
        past_bias = jnp.concatenate(
            [jnp.broadcast_to(suffix[g * HQ + j:g * HQ + j + 1, :], (l, P)) for j in range(HQ)], axis=0)
        kT = ckT_ref[0, g * HQ:(g + 1) * HQ].reshape(W, P).astype(BF16)
        vT = cvT_ref[0, g * HQ:(g + 1) * HQ].reshape(W, P).astype(BF16)
        s_p = _dot(qx, kT) + past_bias + cn_col
        s_n = _dot_nt(qx, kn_ref[0, :, cols].astype(BF16)) + cn_col - cn_keys
        s_n = jnp.where(j_col <= t_row, s_n, -jnp.inf)
        m = jnp.maximum(jnp.max(s_p, axis=1, keepdims=True), jnp.max(s_n, axis=1, keepdims=True))
        p_p = jnp.exp(s_p - m)
        p_n = jnp.exp(s_n - m)
        den = jnp.sum(p_p, axis=1, keepdims=True) + jnp.sum(p_n, axis=1, keepdims=True)
        o_past = _dot_nt(vT, p_p.astype(BF16)).T
        o = (o_past + _dot(p_n.astype(BF16), vn_ref[0, :, cols].astype(BF16))) * (1.0 / den)
        o = jnp.where(own, o, 0.0)
        acc = o[0:l]
        for j in range(1, HQ):
            acc = acc + o[j * l:(j + 1) * l]
        outs.append(acc)
    o_ref[0] = jnp.concatenate(outs, axis=1).astype(BF16)


def _attn_sample(q, kn, vn, lfn, cache_kT, cache_vT, cache_lfT):
    B, l, _ = q.shape
    P = cache_kT.shape[3]
    seq = lambda a, b: pl.BlockSpec((1, a, b), lambda i: (i, 0, 0))
    cache = pl.BlockSpec((1, N_HEADS, HEAD_DIM, P), lambda i: (i, 0, 0, 0))
    return pl.pallas_call(
        functools.partial(_attn_sample_kernel, l=l, P=P),
        grid=(B,),
        in_specs=[seq(l, D_ATTN), seq(l, D_ATTN), seq(l, D_ATTN), seq(l, N_HEADS),
                  cache, cache, seq(N_HEADS, P)],
        out_specs=seq(l, D_ATTN),
        out_shape=jax.ShapeDtypeStruct((B, l, D_ATTN), BF16),
        compiler_params=_params(1),
        name="fox_attn_sample",
    )(q, kn, vn, lfn, cache_kT, cache_vT, cache_lfT)


def _row(v):
    return v.reshape(1, -1).astype(F32)


def _col(v):
    return v.reshape(-1, 1).astype(F32)


def _prep_weights(norm_mix_w, norm_ffn_w, mamba_w_in, mamba_conv_w, mamba_conv_b, mamba_dt_bias,
                  mamba_A_log, mamba_D, mamba_norm_w, mamba_w_out, kv_norm_w, w_k, w_v, w_f, b_f,
                  attn_w_q, attn_w_o, ffn_w_up, ffn_conv_w, ffn_conv_b, ffn_w_down, final_norm_w):
    w_in = mamba_w_in[0]
    w_dt = w_in[:, D_INNER + CONV_DIM:]

    def head_groups(a):
        return jnp.pad(jnp.concatenate([a] * 3, axis=-1), ((0, 0), (0, LANE - 3 * M_HEADS)))

    ex = jnp.repeat(jnp.eye(M_HEADS, dtype=F32), M_HEAD_DIM, axis=1)
    mamba = {
        "norm": _row(norm_mix_w[0]),
        "w_in": jnp.concatenate([w_in[:, :D_INNER + CONV_DIM], head_groups(w_dt)], axis=1).astype(BF16),
        "w_dtT": w_dt.T.astype(BF16),
        "conv_w": mamba_conv_w[0], "conv_b": _row(mamba_conv_b[0]),
        "dt_b": head_groups(_row(mamba_dt_bias[0])), "dt_bT": _col(mamba_dt_bias[0]),
        "a_log": head_groups(_row(mamba_A_log[0])), "a_logT": _col(mamba_A_log[0]),
        "d_x": _row(jnp.repeat(mamba_D[0], M_HEAD_DIM)),
        "gn_w": _row(mamba_norm_w[0]),
        "w_out": mamba_w_out[0].astype(BF16),
        "ex": jnp.pad(jnp.concatenate([ex] * 3, axis=0), ((0, LANE - 3 * M_HEADS), (0, 0))).astype(BF16),
    }
    ffn = [{"norm": _row(norm_ffn_w[i]), "w_up": ffn_w_up[i].astype(BF16), "conv_w": ffn_conv_w[i],
            "conv_b": _row(ffn_conv_b[i]), "w_down": ffn_w_down[i].astype(BF16)} for i in range(2)]

    def place(offset, value):
        m = jnp.zeros((LANE // N_HEADS, N_HEADS, N_HEADS, AUG), F32)
        for i in range(3):
            m = m.at[i, :, :, offset + i].set(value * jnp.eye(N_HEADS, dtype=F32))
        return m.reshape(LANE, N_HEADS * AUG).astype(BF16)

    def three_groups(a):
        return jnp.pad(jnp.concatenate([a] * 3, axis=-1), ((0, 0), (0, LANE - 3 * N_HEADS)))

    def ones_at(offset):
        v = jnp.zeros((N_HEADS, AUG), F32).at[:, offset:offset + 3].set(1.0)
        return v.reshape(1, N_HEADS * AUG)

    w_q = attn_w_q[0] * ATTN_SCALE
    proj = {
        "kv_norm": _row(kv_norm_w), "mix_norm": _row(norm_mix_w[1]),
        "w_k": w_k.astype(BF16), "w_v": w_v.astype(BF16),
        "w_f": jnp.pad(w_f, ((0, 0), (0, LANE - N_HEADS))).astype(BF16), "b_f": _row(b_f),
        "w_f3": three_groups(w_f).astype(BF16), "b_f3": three_groups(_row(b_f)),
        "w_q": w_q.astype(BF16), "w_q_log2": (w_q * LOG2E).astype(BF16),
        "p_k": place(HEAD_DIM, -1.0), "p_q": place(HEAD_DIM + 3, 1.0),
        "one_k": ones_at(HEAD_DIM + 3), "one_q": ones_at(HEAD_DIM),
    }
    return mamba, ffn, proj, attn_w_o[0].astype(BF16), _row(final_norm_w)


MAMBA_TILE = 256
FFN_TILE = 512
PROMPT_CHUNK = 128
ATTN_TILE = 256
ATTN_HEADS = 4
ATTN_UNROLL = 4
BF16_SUBLANES = 16
SAMPLE_SEQS_MAMBA = 4
SAMPLE_SEQS_FFN = 16
SAMPLE_HEADS = 8


def _trunk_prompt(x, weights):
    mamba, ffn, proj, w_o, final_w = weights
    B, Ltot, _ = x.shape
    TM = min(MAMBA_TILE, Ltot)
    T = min(FFN_TILE, Ltot)
    conv0 = jnp.zeros((B, M_CONV_W - 1, CONV_DIM), F32)
    ssm0 = jnp.zeros((B, M_HEADS, M_HEAD_DIM, D_STATE), F32)
    fconv0 = jnp.zeros((B, F_CONV_W - 1, 2 * D_FF), F32)
    h, mconv, ssm = _mamba_layer(x, conv0, ssm0, mamba, nb=1, L=TM, Lc=min(PROMPT_CHUNK, TM))
    h, fconv_a = _ffn_layer(h, fconv0, ffn[0], nb=1, L=T)
    TA = min(ATTN_TILE, Ltot)
    k, v, logf, qa, ka, vT = _proj_prompt(h.reshape(B * Ltot, D_MODEL), proj, T=TA)
    o = _attn_prompt(qa, ka, vT, T=TA)
    y, fconv_b = _ffn_layer(h, fconv0, ffn[1], nb=1, L=T, o=o.reshape(B, Ltot, D_ATTN), w_o=w_o,
                            final_w=final_w)
    return (y, ssm[None], mconv[None], jnp.stack([fconv_a, fconv_b]),
            k.reshape(B, Ltot, N_HEADS, HEAD_DIM), v.reshape(B, Ltot, N_HEADS, HEAD_DIM),
            logf.reshape(B, Ltot, N_HEADS))


def _trunk_sample(x, state_ssm, state_mconv, state_fconv, cache_k, cache_v, cache_logf, weights):
    mamba, ffn, proj, w_o, final_w = weights
    B, l, _ = x.shape
    P = cache_k.shape[1]
    h, mconv, ssm = _mamba_layer(x, state_mconv[0], state_ssm[0], mamba,
                                 nb=min(SAMPLE_SEQS_MAMBA, B), L=l, Lc=l)
    nbf = min(SAMPLE_SEQS_FFN, B)
    h, fconv_a = _ffn_layer(h, state_fconv[0], ffn[0], nb=nbf, L=l)
    k, v, logf, q = _proj_sample(h.reshape(B * l, D_MODEL), proj, T=nbf * l)
    o = _attn_sample(q.reshape(B, l, D_ATTN), k.reshape(B, l, D_ATTN), v.reshape(B, l, D_ATTN),
                     logf.reshape(B, l, N_HEADS), jnp.transpose(cache_k, (0, 2, 3, 1)),
                     jnp.transpose(cache_v, (0, 2, 3, 1)), jnp.swapaxes(cache_logf, 1, 2))
    y, fconv_b = _ffn_layer(h, state_fconv[1], ffn[1], nb=nbf, L=l, o=o, w_o=w_o, final_w=final_w)
    return (y, ssm[None], mconv[None], jnp.stack([fconv_a, fconv_b]),
            k.reshape(B, l, N_HEADS, HEAD_DIM), v.reshape(B, l, N_HEADS, HEAD_DIM),
            logf.reshape(B, l, N_HEADS))


def kernel(x_prompt, x_sample, state_ssm, state_mamba_conv, state_ffn_conv, cache_k, cache_v, cache_logf,
           norm_mix_w, norm_ffn_w, mamba_w_in, mamba_conv_w, mamba_conv_b, mamba_dt_bias, mamba_A_log,
           mamba_D, mamba_norm_w, mamba_w_out, kv_norm_w, w_k, w_v, w_f, b_f, attn_w_q, attn_w_o,
           ffn_w_up, ffn_conv_w, ffn_conv_b, ffn_w_down, final_norm_w):
    weights = _prep_weights(norm_mix_w, norm_ffn_w, mamba_w_in, mamba_conv_w, mamba_conv_b,
                            mamba_dt_bias, mamba_A_log, mamba_D, mamba_norm_w, mamba_w_out, kv_norm_w,
                            w_k, w_v, w_f, b_f, attn_w_q, attn_w_o, ffn_w_up, ffn_conv_w, ffn_conv_b,
                            ffn_w_down, final_norm_w)
    p = _trunk_prompt(x_prompt, weights)
    s = _trunk_sample(x_sample, state_ssm, state_mamba_conv, state_ffn_conv, cache_k, cache_v,
                      cache_logf, weights)
    return (p[0], s[0]) + p[1:] + s[1:]
```

```python
import functools

import jax
import jax.numpy as jnp
from jax import lax
from jax.experimental import pallas as pl
from jax.experimental.pallas import tpu as pltpu

F32 = jnp.float32
BF16 = jnp.bfloat16

EPS = 1e-6
D_MODEL = 1024
D_INNER = 2048
M_HEAD_DIM = 64
M_HEADS = 32
M_GROUPS = 4
HEADS_PER_GROUP = M_HEADS // M_GROUPS
D_STATE = 128
M_CONV_W = 4
CONV_DIM = D_INNER + 2 * M_GROUPS * D_STATE
IN_PROJ_DIM = D_INNER + CONV_DIM + M_HEADS
LANE = 128
IN_PROJ_PAD = -(-IN_PROJ_DIM // LANE) * LANE
HEAD_DIM = 64
N_HEADS = 16
D_ATTN = N_HEADS * HEAD_DIM
ATTN_SCALE = HEAD_DIM ** -0.5
LOG2E = 1.4426950408889634
D_FF = 2816
F_CONV_W = 3
HALO = 8
AUG = 2 * HEAD_DIM
VMEM_LIMIT = 56 * 1024 * 1024


def _dot(a, b):
    return jnp.dot(a, b, preferred_element_type=F32)


def _dot_nt(a, b):
    return lax.dot_general(a, b, (((1,), (1,)), ((), ())), preferred_element_type=F32)


def _dot_tn(a, b):
    return lax.dot_general(a, b, (((0,), (0,)), ((), ())), preferred_element_type=F32)


def _split3(x):
    hi = x.astype(BF16)
    r = x - hi.astype(F32)
    mid = r.astype(BF16)
    lo = (r - mid.astype(F32)).astype(BF16)
    return hi, mid, lo


def _sel_left(sel, x):
    hi, mid, lo = _split3(x)
    return _dot(sel, hi) + _dot(sel, mid) + _dot(sel, lo)


def _sel_right(x, sel):
    hi, mid, lo = _split3(x)
    return _dot(hi, sel) + _dot(mid, sel) + _dot(lo, sel)


def _sel_right_nt(sel, x):
    hi, mid, lo = _split3(x)
    return _dot_nt(sel, hi) + _dot_nt(sel, mid) + _dot_nt(sel, lo)


def _rmsnorm(x, w):
    return x * lax.rsqrt(jnp.mean(x * x, axis=-1, keepdims=True) + EPS) * w


def _silu(x):
    return x * (1.0 / (1.0 + jnp.exp(-x)))


def _softplus(x):
    return jnp.maximum(x, 0.0) + jnp.log1p(jnp.exp(-jnp.abs(x)))


def _tri(n, lower):
    r = lax.broadcasted_iota(jnp.int32, (n, n), 0)
    c = lax.broadcasted_iota(jnp.int32, (n, n), 1)
    return (r >= c) if lower else (r <= c)


def _const_spec(shape):
    nd = len(shape)
    return pl.BlockSpec(shape, lambda *_: (0,) * nd, pipeline_mode=pl.Buffered(1))


def _params(n_axes=2):
    return pltpu.CompilerParams(dimension_semantics=("arbitrary",) * n_axes,
                                vmem_limit_bytes=VMEM_LIMIT)


def _mamba_kernel(x_ref, conv0_ref, ssm0_ref, nw_ref, win_ref, wdtT_ref, cw_ref, cb_ref,
                  dtb_ref, dtbT_ref, alog_ref, alogT_ref, dx_ref, gnw_ref, wout_ref, ex_ref,
                  h_ref, convn_ref, ssmn_ref, fb_ref, y_ref, xbc_ref, *, nb, L, Lc):
    M = nb * L
    lo = HALO - (M_CONV_W - 1)

    @pl.when(pl.program_id(1) == 0)
    def _():
        fb_ref[:, lo:HALO, :] = conv0_ref[...]
        ssmn_ref[...] = ssm0_ref[...]

    x = x_ref[...].reshape(M, D_MODEL)
    ub = _rmsnorm(x, nw_ref[...]).astype(BF16)
    zx = _dot(ub, win_ref[...])
    dtT = _softplus(_dot_nt(wdtT_ref[...], ub) + dtbT_ref[...])
    dt = _softplus(zx[:, D_INNER + CONV_DIM:] + dtb_ref[...])

    fb_ref[:, HALO:HALO + L, :] = zx[:, D_INNER:D_INNER + CONV_DIM].reshape(nb, L, CONV_DIM)
    for c0 in range(0, CONV_DIM, CONV_CHUNK):
        acc = cb_ref[:, c0:c0 + CONV_CHUNK]
        for tap in range(M_CONV_W):
            acc = acc + (fb_ref[:, lo + tap:lo + tap + L, c0:c0 + CONV_CHUNK]
                         * cw_ref[tap:tap + 1, c0:c0 + CONV_CHUNK])
        xbc_ref[:, c0:c0 + CONV_CHUNK] = _silu(acc).reshape(M, CONV_CHUNK)
    halo = fb_ref[:, lo + L:HALO + L, :]
    fb_ref[:, lo:HALO, :] = halo
    convn_ref[...] = halo
    xs = xbc_ref[:, :D_INNER]
    bm = xbc_ref[:, D_INNER:D_INNER + M_GROUPS * D_STATE].astype(BF16)
    cm = xbc_ref[:, D_INNER + M_GROUPS * D_STATE:].astype(BF16)

    dA = dt * (-LOG2E * jnp.exp(alog_ref[...]))
    dAT = dtT * (-LOG2E * jnp.exp(alogT_ref[...]))
    tril = _tri(Lc, True)
    tril_b = tril.astype(BF16)
    triu_b = _tri(Lc, False).astype(BF16)
    ex = ex_ref[...]
    quad = lax.broadcasted_iota(jnp.int32, (Lc, 4 * M_HEAD_DIM), 1) // M_HEAD_DIM
    group = lax.broadcasted_iota(jnp.int32, (Lc, LANE), 1) // M_HEADS

    def expand(a3):
        hi, mid, lo = _split3(a3)
        parts = jnp.where(group == 0, hi, jnp.where(group == 1, mid,
                          jnp.where(group == 2, lo, jnp.zeros_like(lo))))
        return _dot(parts, ex)

    for s in range(nb):
        for c in range(L // Lc):
            r0 = s * L + c * Lc
            acum = _sel_left(tril_b, dA[r0:r0 + Lc])
            dtT_c = dtT[:, r0:r0 + Lc]
            acumT = _sel_right(dAT[:, r0:r0 + Lc], triu_b)
            acum_x = expand(acum)
            dt_x = expand(dt[r0:r0 + Lc])
            xs_c = xs[r0:r0 + Lc]
            x_end = (xs_c * (jnp.exp2(acum_x[Lc - 1:Lc, :] - acum_x) * dt_x)).astype(BF16)
            e_x = jnp.exp2(acum_x)
            d_blk = jnp.exp2(acumT[:, Lc - 1:Lc])
            for g in range(M_GROUPS):
                b_g = bm[r0:r0 + Lc, g * D_STATE:(g + 1) * D_STATE]
                c_g = cm[r0:r0 + Lc, g * D_STATE:(g + 1) * D_STATE]
                cb = _dot_nt(c_g, b_g)
                h0 = g * HEADS_PER_GROUP
                h_in = ssmn_ref[s, h0:h0 + HEADS_PER_GROUP].reshape(HEADS_PER_GROUP * M_HEAD_DIM, D_STATE)
                y_inter = _dot_nt(c_g, h_in.astype(BF16))
                for qd in range(HEADS_PER_GROUP // 4):
                    col = (h0 + 4 * qd) * M_HEAD_DIM
                    xq = xs_c[:, col:col + 4 * M_HEAD_DIM]
                    yq = jnp.zeros((Lc, 4 * M_HEAD_DIM), F32)
                    for j in range(4):
                        h = h0 + 4 * qd + j
                        seg = acum[:, h:h + 1] - acumT[h:h + 1, :]
                        w = cb * jnp.exp2(jnp.where(tril, seg, -jnp.inf)) * dtT_c[h:h + 1, :]
                        yq = yq + _dot(w.astype(BF16), jnp.where(quad == j, xq, 0.0).astype(BF16))
                    y_ref[r0:r0 + Lc, col:col + 4 * M_HEAD_DIM] = (
                        yq + y_inter[:, qd * 4 * M_HEAD_DIM:(qd + 1) * 4 * M_HEAD_DIM]
                        * e_x[:, col:col + 4 * M_HEAD_DIM]
                        + dx_ref[:, col:col + 4 * M_HEAD_DIM] * xs_c[:, col:col + 4 * M_HEAD_DIM])
                st = _dot_tn(x_end[:, h0 * M_HEAD_DIM:(h0 + HEADS_PER_GROUP) * M_HEAD_DIM], b_g)
                for j in range(HEADS_PER_GROUP):
                    h = h0 + j
                    ssmn_ref[s, h] = (ssmn_ref[s, h] * d_blk[h:h + 1, :]
                                      + st[j * M_HEAD_DIM:(j + 1) * M_HEAD_DIM, :])

    yz = y_ref[...] * _silu(zx[:, :D_INNER])
    gs = D_INNER // M_GROUPS
    parts = []
    for g in range(M_GROUPS):
        blk = yz[:, g * gs:(g + 1) * gs]
        parts.append(_rmsnorm(blk, gnw_ref[:, g * gs:(g + 1) * gs]))
    yn = jnp.concatenate(parts, axis=1).astype(BF16)
    h_ref[...] = (x + _dot(yn, wout_ref[...])).reshape(nb, L, D_MODEL)


def _mamba_layer(x, conv0, ssm0, w, *, nb, L, Lc):
    B, Ltot, _ = x.shape
    grid = (B // nb, Ltot // L)
    consts = [w["norm"], w["w_in"], w["w_dtT"], w["conv_w"], w["conv_b"], w["dt_b"], w["dt_bT"],
              w["a_log"], w["a_logT"], w["d_x"], w["gn_w"], w["w_out"], w["ex"]]
    return pl.pallas_call(
        functools.partial(_mamba_kernel, nb=nb, L=L, Lc=Lc),
        grid=grid,
        in_specs=[pl.BlockSpec((nb, L, D_MODEL), lambda s, t: (s, t, 0)),
                  pl.BlockSpec((nb, M_CONV_W - 1, CONV_DIM), lambda s, t: (s, 0, 0)),
                  pl.BlockSpec((nb, M_HEADS, M_HEAD_DIM, D_STATE), lambda s, t: (s, 0, 0, 0))]
        + [_const_spec(c.shape) for c in consts],
        out_specs=[pl.BlockSpec((nb, L, D_MODEL), lambda s, t: (s, t, 0)),
                   pl.BlockSpec((nb, M_CONV_W - 1, CONV_DIM), lambda s, t: (s, 0, 0)),
                   pl.BlockSpec((nb, M_HEADS, M_HEAD_DIM, D_STATE), lambda s, t: (s, 0, 0, 0))],
        out_shape=[jax.ShapeDtypeStruct(x.shape, F32),
                   jax.ShapeDtypeStruct(conv0.shape, F32),
                   jax.ShapeDtypeStruct(ssm0.shape, F32)],
        scratch_shapes=[pltpu.VMEM((nb, HALO + L, CONV_DIM), F32),
                        pltpu.VMEM((nb * L, D_INNER), F32),
                        pltpu.VMEM((nb * L, CONV_DIM), F32)],
        compiler_params=_params(),
        name="mamba_layer",
    )(x, conv0, ssm0, *consts)


def _ffn_kernel(*refs, nb, L, with_attn, final):
    it = iter(refs)
    h_ref = next(it)
    if with_attn:
        o_ref, wo_ref = next(it), next(it)
    fconv0_ref, nw_ref, wup_ref, cw_ref, cb_ref, wdown_ref = (next(it) for _ in range(6))
    if final:
        fnw_ref = next(it)
    out_ref, fconvn_ref, fb_ref = next(it), next(it), next(it)
    M = nb * L
    lo = HALO - (F_CONV_W - 1)

    @pl.when(pl.program_id(1) == 0)
    def _():
        fb_ref[:, lo:HALO, :] = fconv0_ref[...]

    h = h_ref[...].reshape(M, D_MODEL)
    if with_attn:
        h = h + _dot(o_ref[...].reshape(M, D_ATTN), wo_ref[...])
    ub = _rmsnorm(h, nw_ref[...]).astype(BF16)
    fb_ref[:, HALO:HALO + L, :] = _dot(ub, wup_ref[...]).reshape(nb, L, 2 * D_FF)
    acc = cb_ref[...]
    for tap in range(F_CONV_W):
        acc = acc + fb_ref[:, lo + tap:lo + tap + L, :] * cw_ref[tap:tap + 1, :]
    halo = fb_ref[:, lo + L:HALO + L, :]
    fb_ref[:, lo:HALO, :] = halo
    fconvn_ref[...] = halo
    a = acc.reshape(M, 2 * D_FF)
    act = (_silu(a[:, :D_FF]) * a[:, D_FF:]).astype(BF16)
    hn = h + _dot(act, wdown_ref[...])
    if final:
        hn = _rmsnorm(hn, fnw_ref[...])
    out_ref[...] = hn.reshape(nb, L, D_MODEL)


def _ffn_layer(h, fconv0, w, *, nb, L, o=None, w_o=None, final_w=None):
    B, Ltot, _ = h.shape
    grid = (B // nb, Ltot // L)
    with_attn, final = o is not None, final_w is not None
    tile = pl.BlockSpec((nb, L, D_MODEL), lambda s, t: (s, t, 0))
    args, specs = [h], [tile]
    if with_attn:
        args += [o, w_o]
        specs += [pl.BlockSpec((nb, L, D_ATTN), lambda s, t: (s, t, 0)), _const_spec(w_o.shape)]
    consts = [w["norm"], w["w_up"], w["conv_w"], w["conv_b"], w["w_down"]]
    args += [fconv0] + consts
    specs += [pl.BlockSpec((nb, F_CONV_W - 1, 2 * D_FF), lambda s, t: (s, 0, 0))]
    specs += [_const_spec(c.shape) for c in consts]
    if final:
        args.append(final_w)
        specs.append(_const_spec(final_w.shape))
    return pl.pallas_call(
        functools.partial(_ffn_kernel, nb=nb, L=L, with_attn=with_attn, final=final),
        grid=grid,
        in_specs=specs,
        out_specs=[tile, pl.BlockSpec((nb, F_CONV_W - 1, 2 * D_FF), lambda s, t: (s, 0, 0))],
        out_shape=[jax.ShapeDtypeStruct(h.shape, F32), jax.ShapeDtypeStruct(fconv0.shape, F32)],
        scratch_shapes=[pltpu.VMEM((nb, HALO + L, 2 * D_FF), F32)],
        compiler_params=_params(),
        name="conv_ffn",
    )(*args)


def _log_sigmoid(x):
    return -_softplus(-x)


def _proj_prompt_kernel(h_ref, kvw_ref, nmw_ref, wv_ref, wf_ref, bf_ref,
                        wk_ref, wq_ref, pk_ref, pq_ref, onek_ref, oneq_ref,
                        k_ref, v_ref, logf_ref, qa_ref, ka_ref, vT_ref, carry_ref, *, T):
    @pl.when(pl.program_id(0) == 0)
    def _():
        carry_ref[...] = jnp.zeros_like(carry_ref)

    h = h_ref[...]
    hk = _rmsnorm(h, kvw_ref[...]).astype(BF16)
    u1 = _rmsnorm(h, nmw_ref[...]).astype(BF16)
    v = _dot(hk, wv_ref[...])
    v_ref[...] = v
    vT_ref[0] = v.T.astype(BF16)
    k = _dot(hk, wk_ref[...])
    k_ref[...] = k
    low = lax.broadcasted_iota(jnp.int32, (T, AUG), 1) < HEAD_DIM

    def spread(a):
        slots = []
        for j in range(N_HEADS // 2):
            two = a[:, j * AUG:(j + 1) * AUG]
            slots += [jnp.where(low, two, 0.0), jnp.where(low, pltpu.roll(two, HEAD_DIM, axis=1), 0.0)]
        return jnp.concatenate(slots, axis=1)
    logf3 = _log_sigmoid(_dot(hk, wf_ref[...]) + bf_ref[...])
    logf_ref[...] = logf3[:, :N_HEADS]
    c3 = carry_ref[...] + _sel_left(_tri(T, True).astype(BF16), logf3)
    carry_ref[...] = c3[T - 1:T, :]
    c_hi, c_mid, c_lo = _split3(c3 * LOG2E)
    group = lax.broadcasted_iota(jnp.int32, (T, LANE), 1) // N_HEADS
    parts = jnp.where(group == 0, c_hi, jnp.where(group == 1, c_mid,
                      jnp.where(group == 2, c_lo, jnp.zeros_like(c_lo))))
    ka_ref[...] = (spread(k) + onek_ref[...] + _dot(parts, pk_ref[...])).astype(BF16)
    qa_ref[...] = (spread(_dot(u1, wq_ref[...])) + oneq_ref[...] + _dot(parts, pq_ref[...])).astype(BF16)


def _proj_prompt(h2d, w, *, T):
    Ltot = h2d.shape[0]
    consts = [w["kv_norm"], w["mix_norm"], w["w_v"], w["w_f3"], w["b_f3"],
              w["w_k"], w["w_q_log2"], w["p_k"], w["p_q"], w["one_k"], w["one_q"]]
    row = lambda n: pl.BlockSpec((T, n), lambda i: (i, 0))
    return pl.pallas_call(
        functools.partial(_proj_prompt_kernel, T=T),
        grid=(Ltot // T,),
        in_specs=[row(D_MODEL)] + [_const_spec(c.shape) for c in consts],
        out_specs=[row(D_ATTN), row(D_ATTN), row(N_HEADS), row(N_HEADS * AUG), row(N_HEADS * AUG),
                   pl.BlockSpec((1, D_ATTN, T), lambda i: (i, 0, 0))],
        out_shape=[jax.ShapeDtypeStruct((Ltot, D_ATTN), F32),
                   jax.ShapeDtypeStruct((Ltot, D_ATTN), F32),
                   jax.ShapeDtypeStruct((Ltot, N_HEADS), F32),
                   jax.ShapeDtypeStruct((Ltot, N_HEADS * AUG), BF16),
                   jax.ShapeDtypeStruct((Ltot, N_HEADS * AUG), BF16),
                   jax.ShapeDtypeStruct((Ltot // T, D_ATTN, T), BF16)],
        scratch_shapes=[pltpu.VMEM((1, LANE), F32)],
        compiler_params=_params(1),
        name="kvq_proj_prompt",
    )(h2d, *consts)


def _proj_sample_kernel(h_ref, kvw_ref, nmw_ref, wk_ref, wv_ref, wf_ref, bf_ref, wq_ref,
                        k_ref, v_ref, logf_ref, q_ref):
    h = h_ref[...]
    hk = _rmsnorm(h, kvw_ref[...]).astype(BF16)
    u1 = _rmsnorm(h, nmw_ref[...]).astype(BF16)
    k_ref[...] = _dot(hk, wk_ref[...])
    v_ref[...] = _dot(hk, wv_ref[...])
    logf_ref[...] = _log_sigmoid(_dot(hk, wf_ref[...])[:, :N_HEADS] + bf_ref[...])
    q_ref[...] = _dot(u1, wq_ref[...]).astype(BF16)


def _proj_sample(h2d, w, *, T):
    n = h2d.shape[0]
    consts = [w["kv_norm"], w["mix_norm"], w["w_k"], w["w_v"], w["w_f"], w["b_f"], w["w_q"]]
    row = lambda m: pl.BlockSpec((T, m), lambda i: (i, 0))
    return pl.pallas_call(
        _proj_sample_kernel,
        grid=(n // T,),
        in_specs=[row(D_MODEL)] + [_const_spec(c.shape) for c in consts],
        out_specs=[row(D_ATTN), row(D_ATTN), row(N_HEADS), row(D_ATTN)],
        out_shape=[jax.ShapeDtypeStruct((n, D_ATTN), F32), jax.ShapeDtypeStruct((n, D_ATTN), F32),
                   jax.ShapeDtypeStruct((n, N_HEADS), F32), jax.ShapeDtypeStruct((n, D_ATTN), BF16)],
        compiler_params=_params(1),
        name="kvq_proj_sample",
    )(h2d, *consts)


def _attn_prompt_kernel(qa_ref, ka_ref, vT_ref, o_ref, s_ref, acc_ref, oT_ref, *, T, nblk):
    qi = pl.program_id(1)
    heads = range(ATTN_HEADS)
    q = [qa_ref[:, hh * AUG:(hh + 1) * AUG] for hh in heads]
    key_minus_query = (lax.broadcasted_iota(jnp.int32, (T, T), 0)
                       - lax.broadcasted_iota(jnp.int32, (T, T), 1))
    ones = jnp.ones((BF16_SUBLANES, T), BF16)

    def produce(kb, slot):
        start = pl.multiple_of(jnp.minimum(kb, nblk - 1) * T, T)
        bmax = []
        for hh in heads:
            sT = _dot_nt(ka_ref[pl.ds(start, T), hh * AUG:(hh + 1) * AUG], q[hh])
            s_ref[slot, hh] = sT
            bmax.append(jnp.max(sT, axis=0, keepdims=True))
        return tuple(bmax)

    def consume(kb, slot, ms, bmax, masked):
        out = []
        for hh in heads:
            m = ms[hh]
            sT = s_ref[slot, hh]
            if masked:
                sT = jnp.where(key_minus_query <= (qi - kb) * T, sT, -jnp.inf)
                m_new = jnp.maximum(m, jnp.max(sT, axis=0, keepdims=True))
            else:
                m_new = jnp.maximum(m, bmax[hh])
            pT = jnp.exp2(sT - m_new).astype(BF16)
            vT = vT_ref[jnp.minimum(kb, nblk - 1), hh * HEAD_DIM:(hh + 1) * HEAD_DIM, :]
            acc_ref[hh] = jnp.exp2(m - m_new) * acc_ref[hh] + _dot(jnp.concatenate([vT, ones], axis=0), pT)
            out.append(m_new)
        return tuple(out)

    def step(kb, slot, state, masked=False):
        ms, bmax = state
        bmax_next = produce(kb + 1, 1 - slot)
        return consume(kb, slot, ms, bmax, masked), bmax_next

    def pair(i, state):
        return step(2 * i + 1, 1, step(2 * i, 0, state))

    def pairs(i, state):
        for j in range(ATTN_UNROLL):
            state = pair(ATTN_UNROLL * i + j, state)
        return state

    acc_ref[...] = jnp.zeros_like(acc_ref)
    init = tuple(jnp.full((1, T), -jnp.inf, F32) for _ in heads)
    n_pairs = qi // 2
    state = lax.fori_loop(0, n_pairs // ATTN_UNROLL, pairs, (init, produce(0, 0)))
    rest = n_pairs % ATTN_UNROLL
    base = n_pairs - rest
    state = lax.fori_loop(0, rest // 2, lambda i, st: pair(base + 2 * i + 1, pair(base + 2 * i, st)), state)
    state = lax.fori_loop(n_pairs - rest % 2, n_pairs, pair, state)
    last = 2 * n_pairs
    ms, _ = step(last, 0, state, True)

    @pl.when(qi % 2 == 1)
    def _():
        consume(last + 1, 1, ms, None, True)

    for hh in heads:
        acc = acc_ref[hh]
        oT_ref[hh * HEAD_DIM:(hh + 1) * HEAD_DIM, :] = acc[:HEAD_DIM] * (1.0 / acc[HEAD_DIM:HEAD_DIM + 1])
    o_ref[...] = oT_ref[...].T.astype(BF16)


def _attn_prompt(qa, ka, vT, *, T):
    Ltot = qa.shape[0]
    nblk = Ltot // T
    hps = ATTN_HEADS
    return pl.pallas_call(
        functools.partial(_attn_prompt_kernel, T=T, nblk=nblk),
        grid=(N_HEADS // hps, nblk),
        in_specs=[pl.BlockSpec((T, hps * AUG), lambda p, i: (i, p)),
                  pl.BlockSpec((Ltot, hps * AUG), lambda p, i: (0, p)),
                  pl.BlockSpec((nblk, hps * HEAD_DIM, T), lambda p, i: (0, p, 0))],
        out_specs=pl.BlockSpec((T, hps * HEAD_DIM), lambda p, i: (i, p)),
        out_shape=jax.ShapeDtypeStruct((Ltot, D_ATTN), BF16),
        scratch_shapes=[pltpu.VMEM((2, hps, T, T), F32),
                        pltpu.VMEM((hps, HEAD_DIM + BF16_SUBLANES, T), F32),
                        pltpu.VMEM((hps * HEAD_DIM, T), F32)],
        compiler_params=_params(),
        name="fox_attn_prompt",
    )(qa, ka, vT)


def _attn_sample_kernel(q_ref, kn_ref, vn_ref, lfn_ref, ckT_ref, cvT_ref, lfT_ref, o_ref, *, l, P):
    HQ = SAMPLE_HEADS
    R = HQ * l
    W = HQ * HEAD_DIM
    lane_p = lax.broadcasted_iota(jnp.int32, (N_HEADS, P), 1)
    suffix = lfT_ref[0]
    suffix = jnp.where(lane_p < P - 1, pltpu.roll(suffix, P - 1, axis=1), 0.0)
    d = 1
    while d < P:
        suffix = suffix + jnp.where(lane_p < P - d, pltpu.roll(suffix, P - d, axis=1), 0.0)
        d *= 2
    lfn = lfn_ref[0]
    cn = _sel_left(_tri(l, True).astype(BF16), lfn)
    cn_rows = jnp.concatenate([cn] * HQ, axis=0)
    row_head = lax.broadcasted_iota(jnp.int32, (R, N_HEADS), 0) // l
    lane_head = lax.broadcasted_iota(jnp.int32, (R, N_HEADS), 1)
    row_q = lax.broadcasted_iota(jnp.int32, (R, W), 0) // l
    lane_q = lax.broadcasted_iota(jnp.int32, (R, W), 1) // HEAD_DIM
    own = row_q == lane_q
    t_row = lax.broadcasted_iota(jnp.int32, (R, l), 0) % l
    j_col = lax.broadcasted_iota(jnp.int32, (R, l), 1)
    q = q_ref[0]

    outs = []
    for g in range(N_HEADS // HQ):
        cols = slice(g * W, (g + 1) * W)
        sel = lane_head == row_head + g * HQ
        qx = jnp.concatenate([q[:, cols]] * HQ, axis=0)
        qx = jnp.where(own, qx, jnp.zeros_like(qx))
        cn_col = jnp.sum(jnp.where(sel, cn_rows, 0.0), axis=1, keepdims=True)
        cn_keys = _sel_right_nt(sel.astype(BF16), cn)
        past_bias = jnp.concatenate(
            [jnp.broadcast_to(suffix[g * HQ + j:g * HQ + j + 1, :], (l, P)) for j in range(HQ)], axis=0)
        kT = ckT_ref[0, g * HQ:(g + 1) * HQ].reshape(W, P).astype(BF16)
        vT = cvT_ref[0, g * HQ:(g + 1) * HQ].reshape(W, P).astype(BF16)
        s_p = _dot(qx, kT) + past_bias + cn_col
        s_n = _dot_nt(qx, kn_ref[0, :, cols].astype(BF16)) + cn_col - cn_keys
        s_n = jnp.where(j_col <= t_row, s_n, -jnp.inf)
        m = jnp.maximum(jnp.max(s_p, axis=1, keepdims=True), jnp.max(s_n, axis=1, keepdims=True))
        p_p = jnp.exp(s_p - m)
        p_n = jnp.exp(s_n - m)
        den = jnp.sum(p_p, axis=1, keepdims=True) + jnp.sum(p_n, axis=1, keepdims=True)
        o_past = _dot_nt(vT, p_p.astype(BF16)).T
        o = (o_past + _dot(p_n.astype(BF16), vn_ref[0, :, cols].astype(BF16))) * (1.0 / den)
        o = jnp.where(own, o, 0.0)
        acc = o[0:l]
        for j in range(1, HQ):
            acc = acc + o[j * l:(j + 1) * l]
        outs.append(acc)
    o_ref[0] = jnp.concatenate(outs, axis=1).astype(BF16)


def _attn_sample(q, kn, vn, lfn, cache_kT, cache_vT, cache_lfT):
    B, l, _ = q.shape
    P = cache_kT.shape[3]
    seq = lambda a, b: pl.BlockSpec((1, a, b), lambda i: (i, 0, 0))
    cache = pl.BlockSpec((1, N_HEADS, HEAD_DIM, P), lambda i: (i, 0, 0, 0))
    return pl.pallas_call(
        functools.partial(_attn_sample_kernel, l=l, P=P),
        grid=(B,),
        in_specs=[seq(l, D_ATTN), seq(l, D_ATTN), seq(l, D_ATTN), seq(l, N_HEADS),
                  cache, cache, seq(N_HEADS, P)],
        out_specs=seq(l, D_ATTN),
        out_shape=jax.ShapeDtypeStruct((B, l, D_ATTN), BF16),
        compiler_params=_params(1),
        name="fox_attn_sample",
    )(q, kn, vn, lfn, cache_kT, cache_vT, cache_lfT)


def _row(v):
    return v.reshape(1, -1).astype(F32)


def _col(v):
    return v.reshape(-1, 1).astype(F32)


def _prep_weights(norm_mix_w, norm_ffn_w, mamba_w_in, mamba_conv_w, mamba_conv_b, mamba_dt_bias,
                  mamba_A_log, mamba_D, mamba_norm_w, mamba_w_out, kv_norm_w, w_k, w_v, w_f, b_f,
                  attn_w_q, attn_w_o, ffn_w_up, ffn_conv_w, ffn_conv_b, ffn_w_down, final_norm_w):
    w_in = mamba_w_in[0]
    w_dt = w_in[:, D_INNER + CONV_DIM:]

    def head_groups(a):
        return jnp.pad(jnp.concatenate([a] * 3, axis=-1), ((0, 0), (0, LANE - 3 * M_HEADS)))

    ex = jnp.repeat(jnp.eye(M_HEADS, dtype=F32), M_HEAD_DIM, axis=1)
    mamba = {
        "norm": _row(norm_mix_w[0]),
        "w_in": jnp.concatenate([w_in[:, :D_INNER + CONV_DIM], head_groups(w_dt)], axis=1).astype(BF16),
        "w_dtT": w_dt.T.astype(BF16),
        "conv_w": mamba_conv_w[0], "conv_b": _row(mamba_conv_b[0]),
        "dt_b": head_groups(_row(mamba_dt_bias[0])), "dt_bT": _col(mamba_dt_bias[0]),
        "a_log": head_groups(_row(mamba_A_log[0])), "a_logT": _col(mamba_A_log[0]),
        "d_x": _row(jnp.repeat(mamba_D[0], M_HEAD_DIM)),
        "gn_w": _row(mamba_norm_w[0]),
        "w_out": mamba_w_out[0].astype(BF16),
        "ex": jnp.pad(jnp.concatenate([ex] * 3, axis=0), ((0, LANE - 3 * M_HEADS), (0, 0))).astype(BF16),
    }
    ffn = [{"norm": _row(norm_ffn_w[i]), "w_up": ffn_w_up[i].astype(BF16), "conv_w": ffn_conv_w[i],
            "conv_b": _row(ffn_conv_b[i]), "w_down": ffn_w_down[i].astype(BF16)} for i in range(2)]

    def place(offset, value):
        m = jnp.zeros((LANE // N_HEADS, N_HEADS, N_HEADS, AUG), F32)
        for i in range(3):
            m = m.at[i, :, :, offset + i].set(value * jnp.eye(N_HEADS, dtype=F32))
        return m.reshape(LANE, N_HEADS * AUG).astype(BF16)

    def three_groups(a):
        return jnp.pad(jnp.concatenate([a] * 3, axis=-1), ((0, 0), (0, LANE - 3 * N_HEADS)))

    def ones_at(offset):
        v = jnp.zeros((N_HEADS, AUG), F32).at[:, offset:offset + 3].set(1.0)
        return v.reshape(1, N_HEADS * AUG)

    w_q = attn_w_q[0] * ATTN_SCALE
    proj = {
        "kv_norm": _row(kv_norm_w), "mix_norm": _row(norm_mix_w[1]),
        "w_k": w_k.astype(BF16), "w_v": w_v.astype(BF16),
        "w_f": jnp.pad(w_f, ((0, 0), (0, LANE - N_HEADS))).astype(BF16), "b_f": _row(b_f),
        "w_f3": three_groups(w_f).astype(BF16), "b_f3": three_groups(_row(b_f)),
        "w_q": w_q.astype(BF16), "w_q_log2": (w_q * LOG2E).astype(BF16),
        "p_k": place(HEAD_DIM, -1.0), "p_q": place(HEAD_DIM + 3, 1.0),
        "one_k": ones_at(HEAD_DIM + 3), "one_q": ones_at(HEAD_DIM),
    }
    return mamba, ffn, proj, attn_w_o[0].astype(BF16), _row(final_norm_w)


MAMBA_TILE = 256
FFN_TILE = 512
PROMPT_CHUNK = 128
CONV_CHUNK = 512
ATTN_TILE = 256
ATTN_HEADS = 4
ATTN_UNROLL = 4
BF16_SUBLANES = 16
SAMPLE_SEQS_MAMBA = 4
SAMPLE_SEQS_FFN = 16
SAMPLE_HEADS = 8


def _trunk_prompt(x, weights):
    mamba, ffn, proj, w_o, final_w = weights
    B, Ltot, _ = x.shape
    TM = min(MAMBA_TILE, Ltot)
    T = min(FFN_TILE, Ltot)
    conv0 = jnp.zeros((B, M_CONV_W - 1, CONV_DIM), F32)
    ssm0 = jnp.zeros((B, M_HEADS, M_HEAD_DIM, D_STATE), F32)
    fconv0 = jnp.zeros((B, F_CONV_W - 1, 2 * D_FF), F32)
    h, mconv, ssm = _mamba_layer(x, conv0, ssm0, mamba, nb=1, L=TM, Lc=min(PROMPT_CHUNK, TM))
    h, fconv_a = _ffn_layer(h, fconv0, ffn[0], nb=1, L=T)
    TA = min(ATTN_TILE, Ltot)
    k, v, logf, qa, ka, vT = _proj_prompt(h.reshape(B * Ltot, D_MODEL), proj, T=TA)
    o = _attn_prompt(qa, ka, vT, T=TA)
    y, fconv_b = _ffn_layer(h, fconv0, ffn[1], nb=1, L=T, o=o.reshape(B, Ltot, D_ATTN), w_o=w_o,
                            final_w=final_w)
    return (y, ssm[None], mconv[None], jnp.stack([fconv_a, fconv_b]),
            k.reshape(B, Ltot, N_HEADS, HEAD_DIM), v.reshape(B, Ltot, N_HEADS, HEAD_DIM),
            logf.reshape(B, Ltot, N_HEADS))


def _trunk_sample(x, state_ssm, state_mconv, state_fconv, cache_k, cache_v, cache_logf, weights):
    mamba, ffn, proj, w_o, final_w = weights
    B, l, _ = x.shape
    P = cache_k.shape[1]
    h, mconv, ssm = _mamba_layer(x, state_mconv[0], state_ssm[0], mamba,
                                 nb=min(SAMPLE_SEQS_MAMBA, B), L=l, Lc=l)
    nbf = min(SAMPLE_SEQS_FFN, B)
    h, fconv_a = _ffn_layer(h, state_fconv[0], ffn[0], nb=nbf, L=l)
    k, v, logf, q = _proj_sample(h.reshape(B * l, D_MODEL), proj, T=nbf * l)
    o = _attn_sample(q.reshape(B, l, D_ATTN), k.reshape(B, l, D_ATTN), v.reshape(B, l, D_ATTN),
                     logf.reshape(B, l, N_HEADS), jnp.transpose(cache_k, (0, 2, 3, 1)),
                     jnp.transpose(cache_v, (0, 2, 3, 1)), jnp.swapaxes(cache_logf, 1, 2))
    y, fconv_b = _ffn_layer(h, state_fconv[1], ffn[1], nb=nbf, L=l, o=o, w_o=w_o, final_w=final_w)
    return (y, ssm[None], mconv[None], jnp.stack([fconv_a, fconv_b]),
            k.reshape(B, l, N_HEADS, HEAD_DIM), v.reshape(B, l, N_HEADS, HEAD_DIM),
            logf.reshape(B, l, N_HEADS))


def kernel(x_prompt, x_sample, state_ssm, state_mamba_conv, state_ffn_conv, cache_k, cache_v, cache_logf,
           norm_mix_w, norm_ffn_w, mamba_w_in, mamba_conv_w, mamba_conv_b, mamba_dt_bias, mamba_A_log,
           mamba_D, mamba_norm_w, mamba_w_out, kv_norm_w, w_k, w_v, w_f, b_f, attn_w_q, attn_w_o,
           ffn_w_up, ffn_conv_w, ffn_conv_b, ffn_w_down, final_norm_w):
    weights = _prep_weights(norm_mix_w, norm_ffn_w, mamba_w_in, mamba_conv_w, mamba_conv_b,
                            mamba_dt_bias, mamba_A_log, mamba_D, mamba_norm_w, mamba_w_out, kv_norm_w,
                            w_k, w_v, w_f, b_f, attn_w_q, attn_w_o, ffn_w_up, ffn_conv_w, ffn_conv_b,
                            ffn_w_down, final_norm_w)
    p = _trunk_prompt(x_prompt, weights)
    s = _trunk_sample(x_sample, state_ssm, state_mamba_conv, state_ffn_conv, cache_k, cache_v,
                      cache_logf, weights)
    return (p[0], s[0]) + p[1:] + s[1:]
```
